```python
import math
import jax
import jax.numpy as jnp
from jax import lax
import numpy as np

D_MODEL = 1024
BATCH = 8
SEQ = 4096
DEPTH = 2

CTX_LEN = 256
GRID_W = 64
RMS_EPS = 1e-6

MLA_HEADS = 8
MLA_Q_LORA = 384
MLA_KV_LORA = 256
MLA_NOPE = 64
MLA_ROPE = 32
MLA_V = 64
MLA_QK = MLA_NOPE + MLA_ROPE
ROPE_PAIRS = MLA_ROPE // 4
ROPE_BASE = 10000.0
Q_BLOCK = 128

LRU_WIDTH = 512
LRU_BLOCKS = 8
LRU_BLOCK_DIM = LRU_WIDTH // LRU_BLOCKS
LRU_C = 8.0
LRU_CONV = 4

EVEN_SPLITS = (MLA_Q_LORA, MLA_Q_LORA + MLA_KV_LORA, MLA_Q_LORA + MLA_KV_LORA + MLA_ROPE, MLA_Q_LORA + MLA_KV_LORA + MLA_ROPE + LRU_WIDTH)
EVEN_IN = MLA_Q_LORA + MLA_KV_LORA + MLA_ROPE + 2 * LRU_WIDTH
EVEN_MIX = MLA_HEADS * MLA_V + LRU_WIDTH

HY_WIDTH = D_MODEL
HY_ORDER = 2
HY_CONV = 3
HY_BANDS = 16
HY_EMB = 2 * HY_BANDS + 1
HY_HIDDEN = 64

N_EXPERTS = 16
N_GROUPS = 4
EXPERTS_PER_GROUP = N_EXPERTS // N_GROUPS
TOP_K = 2
EXPERT_FF = 512

N_EVEN = (DEPTH + 1) // 2
N_ODD = DEPTH // 2

kernel_name = 'hybrid_mla_rglru_hyena_moe_dit'


def rms_norm(x, g):
    xf = x.astype(jnp.float32)
    y = xf * lax.rsqrt(jnp.mean(xf * xf, axis=-1, keepdims=True) + RMS_EPS)
    return (y * g.astype(jnp.float32)).astype(x.dtype)


def modulate(h, shift, scale):
    return h * (1.0 + scale) + shift


def grid_rope_tables(rows):
    row = jnp.repeat(jnp.arange(rows), GRID_W)
    col = jnp.tile(jnp.arange(GRID_W), rows)
    inv_freq = ROPE_BASE ** (-jnp.arange(ROPE_PAIRS, dtype=jnp.float32) / ROPE_PAIRS)
    ang = jnp.stack([row, col], axis=-1).astype(jnp.float32)[:, :, None] * inv_freq
    return jnp.cos(ang), jnp.sin(ang)


def apply_rope_2d(x, cos, sin):
    xr = x.astype(jnp.float32).reshape(*x.shape[:-1], 2, 2, ROPE_PAIRS)
    xa, xb = xr[..., 0, :], xr[..., 1, :]
    out = jnp.stack([xa * cos - xb * sin, xa * sin + xb * cos], axis=-2)
    return out.reshape(x.shape).astype(x.dtype)


def attend(q, k, v):
    s = jnp.einsum('bqhd,bkhd->bhqk', q, k).astype(jnp.float32) * (MLA_QK ** -0.5)
    p = jax.nn.softmax(s, axis=-1).astype(v.dtype)
    return jnp.einsum('bhqk,bkhd->bqhd', p, v)


def blocked_attend(q, k, v):
    bsz, n, heads, dk = q.shape
    nb = n // Q_BLOCK
    qb = q.reshape(bsz, nb, Q_BLOCK, heads, dk).transpose(1, 0, 2, 3, 4)
    ob = lax.map(lambda qi: attend(qi, k, v), qb)
    return ob.transpose(1, 0, 2, 3, 4).reshape(bsz, n, heads, v.shape[-1])


def depthwise_conv(x, w, b, pad_left, pad_right):
    y = lax.conv_general_dilated(x, w[:, None, :].astype(x.dtype), window_strides=(1,),
                                 padding=[(pad_left, pad_right)],
                                 dimension_numbers=('NWC', 'WIO', 'NWC'),
                                 feature_group_count=x.shape[-1])
    return y + b.astype(x.dtype)


def linear_scan(a, b, h0):
    def combine(left, right):
        a_l, b_l = left
        a_r, b_r = right
        return a_l * a_r, a_r * b_l + b_r
    a_cum, b_cum = lax.associative_scan(combine, (a, b), axis=1)
    return a_cum * h0[:, None, :] + b_cum


def rglru_coeffs(u, w_a, b_a, w_x, b_x, lam):
    bsz, n, width = u.shape
    ub = u.reshape(bsz, n, LRU_BLOCKS, LRU_BLOCK_DIM)
    gate_a = jnp.einsum('bnhi,hij->bnhj', ub, w_a).reshape(bsz, n, width) + b_a
    gate_x = jnp.einsum('bnhi,hij->bnhj', ub, w_x).reshape(bsz, n, width) + b_x
    r = jax.nn.sigmoid(gate_a.astype(jnp.float32))
    i = jax.nn.sigmoid(gate_x.astype(jnp.float32))
    log_a = -LRU_C * r * jax.nn.softplus(-lam.astype(jnp.float32))
    a = jnp.exp(log_a)
    b = jnp.sqrt(-jnp.expm1(2.0 * log_a)) * (i * u.astype(jnp.float32))
    return a, b


def maybe_flip(t, reverse):
    return jnp.flip(t, axis=1) if reverse else t


def rglru_bidirectional(u_ctx, u_lat, w_a, b_a, w_x, b_x, lam):
    zero = jnp.zeros((u_lat.shape[0], u_lat.shape[-1]), jnp.float32)
    outs = []
    for d in range(2):
        rev = d == 1
        a_c, b_c = rglru_coeffs(maybe_flip(u_ctx, rev), w_a[d], b_a[d], w_x[d], b_x[d], lam[d])
        h_c = linear_scan(a_c, b_c, zero)
        a_l, b_l = rglru_coeffs(maybe_flip(u_lat, rev), w_a[d], b_a[d], w_x[d], b_x[d], lam[d])
        h_l = linear_scan(a_l, b_l, h_c[:, -1])
        outs.append((maybe_flip(h_c, rev), maybe_flip(h_l, rev)))
    return outs[0][0] + outs[1][0], outs[0][1] + outs[1][1]


def mla_up(cq, ckv, q_norm_g, kv_norm_g, w_uq, w_ukv):
    lead = cq.shape[:-1]
    q = (rms_norm(cq, q_norm_g) @ w_uq).reshape(*lead, MLA_HEADS, MLA_QK)
    kv = (rms_norm(ckv, kv_norm_g) @ w_ukv).reshape(*lead, MLA_HEADS, MLA_NOPE + MLA_V)
    return q, kv[..., :MLA_NOPE], kv[..., MLA_NOPE:]


def mla_rglru_mixer(h_ctx, h_lat, cos, sin, w_in, q_norm_g, kv_norm_g, w_uq, w_ukv,
                    conv_w, conv_b, w_a, b_a, w_x, b_x, lam, w_out, need_ctx):
    bsz, n_lat, _ = h_lat.shape
    n_ctx = h_ctx.shape[1]
    cq_c, ckv_c, kr_c, ux_c, ug_c = jnp.split(h_ctx @ w_in, EVEN_SPLITS, axis=-1)
    cq_l, ckv_l, kr_l, ux_l, ug_l = jnp.split(h_lat @ w_in, EVEN_SPLITS, axis=-1)

    q_c, kn_c, v_c = mla_up(cq_c, ckv_c, q_norm_g, kv_norm_g, w_uq, w_ukv)
    q_l, kn_l, v_l = mla_up(cq_l, ckv_l, q_norm_g, kv_norm_g, w_uq, w_ukv)
    k_c = jnp.concatenate([kn_c, jnp.broadcast_to(kr_c[:, :, None, :], (bsz, n_ctx, MLA_HEADS, MLA_ROPE))], axis=-1)
    kr_l = apply_rope_2d(kr_l, cos, sin)
    k_l = jnp.concatenate([kn_l, jnp.broadcast_to(kr_l[:, :, None, :], (bsz, n_lat, MLA_HEADS, MLA_ROPE))], axis=-1)
    q_l = jnp.concatenate([q_l[..., :MLA_NOPE], apply_rope_2d(q_l[..., MLA_NOPE:], cos[:, None], sin[:, None])], axis=-1)
    k_all = jnp.concatenate([k_c, k_l], axis=1)
    v_all = jnp.concatenate([v_c, v_l], axis=1)
    att_l = blocked_attend(q_l, k_all, v_all).reshape(bsz, n_lat, MLA_HEADS * MLA_V)

    u_c = depthwise_conv(ux_c, conv_w, conv_b, 2, 1)
    u_l = depthwise_conv(ux_l, conv_w, conv_b, 2, 1)
    y_c, y_l = rglru_bidirectional(u_c, u_l, w_a, b_a, w_x, b_x, lam)
    lru_l = (y_l * jax.nn.gelu(ug_l.astype(jnp.float32))).astype(h_lat.dtype)
    out_l = jnp.concatenate([att_l, lru_l], axis=-1) @ w_out
    if not need_ctx:
        return None, out_l
    att_c = attend(q_c, k_c, v_c).reshape(bsz, n_ctx, MLA_HEADS * MLA_V)
    lru_c = (y_c * jax.nn.gelu(ug_c.astype(jnp.float32))).astype(h_ctx.dtype)
    out_c = jnp.concatenate([att_c, lru_c], axis=-1) @ w_out
    return out_c, out_l


def hyena_filters(n, w1, b1, f1, w2, b2, f2, w3, decay):
    t = jnp.linspace(0.0, 1.0, n, dtype=jnp.float32)[:, None]
    w = (2.0 * math.pi / n) * jnp.arange(n, dtype=jnp.float32)[:, None]
    f = jnp.linspace(1e-4, HY_BANDS - 1, HY_BANDS, dtype=jnp.float32)[None, :]
    z = jnp.concatenate([t, jnp.cos(f * w), -jnp.sin(f * w)], axis=-1)
    a = jnp.sin(f1.astype(jnp.float32) * (z @ w1.astype(jnp.float32) + b1.astype(jnp.float32)))
    a = jnp.sin(f2.astype(jnp.float32) * (a @ w2.astype(jnp.float32) + b2.astype(jnp.float32)))
    h = (a @ w3.astype(jnp.float32)).reshape(n, 2 * HY_ORDER, HY_WIDTH)
    h = h * jnp.exp(-t[:, :, None] * jnp.abs(decay.astype(jnp.float32)))
    h = h.reshape(n, HY_ORDER, 2, HY_WIDTH)
    return h / (jnp.sum(jnp.abs(h), axis=(0, 2), keepdims=True) + 1e-6)


def two_sided_fftconv(u, h_fwd, h_bwd, skip):
    n = u.shape[1]
    u32 = u.astype(jnp.float32)
    spec_u = jnp.fft.rfft(u32, n=2 * n, axis=1)
    spec_h = jnp.fft.rfft(h_fwd, n=2 * n, axis=0) + jnp.conj(jnp.fft.rfft(h_bwd, n=2 * n, axis=0))
    y = jnp.fft.irfft(spec_u * spec_h[None], n=2 * n, axis=1)[:, :n]
    return (y + u32 * skip.astype(jnp.float32)).astype(u.dtype)


def hyena_operator(h, w_in, conv_w, conv_b, w1, b1, f1, w2, b2, f2, w3, decay, skip, w_out):
    n = h.shape[1]
    p = depthwise_conv(h @ w_in, conv_w, conv_b, 1, 1)
    v, x1, x2 = jnp.split(p, 3, axis=-1)
    filt = hyena_filters(n, w1, b1, f1, w2, b2, f2, w3, decay)
    z = v
    for order, gate in enumerate((x1, x2)):
        z = gate * two_sided_fftconv(z, filt[:, order, 0], filt[:, order, 1], skip[order])
    return z @ w_out


def grouped_moe(h, router_w, router_b, w1, w3, w2):
    s = jax.nn.sigmoid(jnp.einsum('bld,de->ble', h, router_w).astype(jnp.float32))
    sel = s + router_b.astype(jnp.float32)
    sel_g = sel.reshape(*sel.shape[:-1], N_GROUPS, EXPERTS_PER_GROUP)
    group_score = jnp.sum(lax.top_k(sel_g, 2)[0], axis=-1)
    g_idx = jnp.argmax(group_score, axis=-1)
    in_group = (jnp.arange(N_EXPERTS) // EXPERTS_PER_GROUP) == g_idx[..., None]
    _, e_idx = lax.top_k(jnp.where(in_group, sel, -jnp.inf), TOP_K)
    w_sel = jnp.take_along_axis(s, e_idx, axis=-1)
    w_sel = w_sel / jnp.sum(w_sel, axis=-1, keepdims=True)
    gates = jnp.sum(jax.nn.one_hot(e_idx, N_EXPERTS, dtype=jnp.float32) * w_sel[..., None], axis=-2).astype(h.dtype)
    y = jnp.zeros_like(h)
    for e in range(N_EXPERTS):
        act = jax.nn.silu(h @ w1[e]) * (h @ w3[e])
        y = y + gates[..., e:e + 1] * (act @ w2[e])
    return y


def setup_inputs(seed: int = 0) -> dict:
    key = jax.random.key(seed)
    keys = iter(jax.random.split(key, 64))

    def normal(shape, scale):
        return jax.random.normal(next(keys), shape, jnp.float32) * scale

    def gain(shape):
        return 1.0 + normal(shape, 0.02)

    d = D_MODEL
    u = jax.random.uniform(next(keys), (N_EVEN, 2, LRU_WIDTH), jnp.float32, 0.9, 0.999)
    a_base = u ** (1.0 / LRU_C)
    lru_lambda = jnp.log(a_base) - jnp.log1p(-a_base)
    decay_base = jnp.linspace(math.log(100.0) / 1.5, math.log(100.0) / 0.3, HY_WIDTH, dtype=jnp.float32)
    hy_decay = decay_base * (1.0 + normal((N_ODD, 2 * HY_ORDER, HY_WIDTH), 0.05))
    return {
        'x': normal((BATCH, SEQ, d), 1.0),
        'c': normal((BATCH, d), 1.0),
        'ctx': normal((BATCH, CTX_LEN, d), 1.0),
        'c_ctx': normal((d,), 1.0),
        'ada_w': normal((DEPTH, d, 6 * d), 0.5 * d ** -0.5),
        'ada_b': normal((DEPTH, 6 * d), 0.02),
        'norm1_g': gain((DEPTH, d)),
        'norm2_g': gain((DEPTH, d)),
        'final_g': gain((d,)),
        'ev_w_in': normal((N_EVEN, d, EVEN_IN), d ** -0.5),
        'mla_q_norm_g': gain((N_EVEN, MLA_Q_LORA)),
        'mla_kv_norm_g': gain((N_EVEN, MLA_KV_LORA)),
        'mla_w_uq': normal((N_EVEN, MLA_Q_LORA, MLA_HEADS * MLA_QK), MLA_Q_LORA ** -0.5),
        'mla_w_ukv': normal((N_EVEN, MLA_KV_LORA, MLA_HEADS * (MLA_NOPE + MLA_V)), MLA_KV_LORA ** -0.5),
        'lru_conv_w': normal((N_EVEN, LRU_CONV, LRU_WIDTH), LRU_CONV ** -0.5),
        'lru_conv_b': normal((N_EVEN, LRU_WIDTH), 0.02),
        'lru_w_a': normal((N_EVEN, 2, LRU_BLOCKS, LRU_BLOCK_DIM, LRU_BLOCK_DIM), LRU_BLOCK_DIM ** -0.5),
        'lru_b_a': normal((N_EVEN, 2, LRU_WIDTH), 0.02),
        'lru_w_x': normal((N_EVEN, 2, LRU_BLOCKS, LRU_BLOCK_DIM, LRU_BLOCK_DIM), LRU_BLOCK_DIM ** -0.5),
        'lru_b_x': normal((N_EVEN, 2, LRU_WIDTH), 0.02),
        'lru_lambda': lru_lambda,
        'ev_w_out': normal((N_EVEN, EVEN_MIX, d), EVEN_MIX ** -0.5),
        'od_w_in': normal((N_ODD, d, 3 * HY_WIDTH), d ** -0.5),
        'hy_conv_w': normal((N_ODD, HY_CONV, 3 * HY_WIDTH), HY_CONV ** -0.5),
        'hy_conv_b': normal((N_ODD, 3 * HY_WIDTH), 0.02),
        'hy_w1': normal((N_ODD, HY_EMB, HY_HIDDEN), HY_EMB ** -0.5),
        'hy_b1': normal((N_ODD, HY_HIDDEN), 0.1),
        'hy_freq1': 1.0 + normal((N_ODD, HY_HIDDEN), 0.05),
        'hy_w2': normal((N_ODD, HY_HIDDEN, HY_HIDDEN), HY_HIDDEN ** -0.5),
        'hy_b2': normal((N_ODD, HY_HIDDEN), 0.1),
        'hy_freq2': 1.0 + normal((N_ODD, HY_HIDDEN), 0.05),
        'hy_w3': normal((N_ODD, HY_HIDDEN, 2 * HY_ORDER * HY_WIDTH), HY_HIDDEN ** -0.5),
        'hy_decay': hy_decay,
        'hy_skip': normal((N_ODD, HY_ORDER, HY_WIDTH), 1.0),
        'od_w_out': normal((N_ODD, HY_WIDTH, d), HY_WIDTH ** -0.5),
        'router_w': normal((d, N_EXPERTS), d ** -0.5),
        'router_b': normal((N_EXPERTS,), 0.01),
        'moe_w1': normal((DEPTH, N_EXPERTS, d, EXPERT_FF), d ** -0.5),
        'moe_w3': normal((DEPTH, N_EXPERTS, d, EXPERT_FF), d ** -0.5),
        'moe_w2': normal((DEPTH, N_EXPERTS, EXPERT_FF, d), EXPERT_FF ** -0.5),
    }


def reference(x, c, ctx, c_ctx, ada_w, ada_b, norm1_g, norm2_g, final_g,
              ev_w_in, mla_q_norm_g, mla_kv_norm_g, mla_w_uq, mla_w_ukv,
              lru_conv_w, lru_conv_b, lru_w_a, lru_b_a, lru_w_x, lru_b_x, lru_lambda, ev_w_out,
              od_w_in, hy_conv_w, hy_conv_b, hy_w1, hy_b1, hy_freq1, hy_w2, hy_b2, hy_freq2,
              hy_w3, hy_decay, hy_skip, od_w_out,
              router_w, router_b, moe_w1, moe_w3, moe_w2):
    n_lat = x.shape[1]
    n_ctx = ctx.shape[1]
    rows = n_lat // GRID_W
    cos, sin = grid_rope_tables(rows)
    silu_c = jax.nn.silu(c)
    silu_cc = jax.nn.silu(c_ctx)
    xc = ctx
    for layer in range(DEPTH):
        need_ctx = layer < DEPTH - 1
        is_even = layer % 2 == 0
        idx = layer // 2
        mod = (silu_c @ ada_w[layer] + ada_b[layer])[:, None, :]
        sh1, sc1, g1, sh2, sc2, g2 = jnp.split(mod, 6, axis=-1)
        h_lat = modulate(rms_norm(x, norm1_g[layer]), sh1, sc1)
        h_ctx = None
        if need_ctx or is_even:
            mod_c = silu_cc @ ada_w[layer] + ada_b[layer]
            csh1, csc1, cg1, csh2, csc2, cg2 = jnp.split(mod_c, 6, axis=-1)
            h_ctx = modulate(rms_norm(xc, norm1_g[layer]), csh1, csc1)
        if is_even:
            o_ctx, o_lat = mla_rglru_mixer(h_ctx, h_lat, cos, sin, ev_w_in[idx], mla_q_norm_g[idx],
                                           mla_kv_norm_g[idx], mla_w_uq[idx], mla_w_ukv[idx],
                                           lru_conv_w[idx], lru_conv_b[idx], lru_w_a[idx], lru_b_a[idx],
                                           lru_w_x[idx], lru_b_x[idx], lru_lambda[idx], ev_w_out[idx],
                                           need_ctx)
        else:
            hy_params = (od_w_in[idx], hy_conv_w[idx], hy_conv_b[idx], hy_w1[idx], hy_b1[idx],
                         hy_freq1[idx], hy_w2[idx], hy_b2[idx], hy_freq2[idx], hy_w3[idx],
                         hy_decay[idx], hy_skip[idx], od_w_out[idx])
            o_lat = hyena_operator(h_lat, *hy_params)
            o_ctx = hyena_operator(h_ctx, *hy_params) if need_ctx else None
        x = x + g1 * o_lat
        h_lat = modulate(rms_norm(x, norm2_g[layer]), sh2, sc2)
        if need_ctx:
            xc = xc + cg1 * o_ctx
            h_ctx = modulate(rms_norm(xc, norm2_g[layer]), csh2, csc2)
            m = grouped_moe(jnp.concatenate([h_ctx, h_lat], axis=1), router_w, router_b,
                            moe_w1[layer], moe_w3[layer], moe_w2[layer])
            xc = xc + cg2 * m[:, :n_ctx]
            m_lat = m[:, n_ctx:]
        else:
            m_lat = grouped_moe(h_lat, router_w, router_b, moe_w1[layer], moe_w3[layer], moe_w2[layer])
        x = x + g2 * m_lat
    return rms_norm(x, final_g)
```

```python
import math
import functools
import jax
import jax.numpy as jnp
from jax import lax
import numpy as np
from jax.experimental import pallas as pl
from jax.experimental.pallas import tpu as pltpu

D_MODEL = 1024
BATCH = 8
SEQ = 4096
DEPTH = 2

CTX_LEN = 256
GRID_W = 64
RMS_EPS = 1e-6

MLA_HEADS = 8
MLA_Q_LORA = 384
MLA_KV_LORA = 256
MLA_NOPE = 64
MLA_ROPE = 32
MLA_V = 64
MLA_QK = MLA_NOPE + MLA_ROPE
ROPE_PAIRS = MLA_ROPE // 4
ROPE_BASE = 10000.0
Q_BLOCK = 128

LRU_WIDTH = 512
LRU_BLOCKS = 8
LRU_BLOCK_DIM = LRU_WIDTH // LRU_BLOCKS
LRU_C = 8.0
LRU_CONV = 4

EVEN_SPLITS = (MLA_Q_LORA, MLA_Q_LORA + MLA_KV_LORA, MLA_Q_LORA + MLA_KV_LORA + MLA_ROPE, MLA_Q_LORA + MLA_KV_LORA + MLA_ROPE + LRU_WIDTH)
EVEN_IN = MLA_Q_LORA + MLA_KV_LORA + MLA_ROPE + 2 * LRU_WIDTH
EVEN_MIX = MLA_HEADS * MLA_V + LRU_WIDTH

HY_WIDTH = D_MODEL
HY_ORDER = 2
HY_CONV = 3
HY_BANDS = 16
HY_EMB = 2 * HY_BANDS + 1
HY_HIDDEN = 64

N_EXPERTS = 16
N_GROUPS = 4
EXPERTS_PER_GROUP = N_EXPERTS // N_GROUPS
TOP_K = 2
EXPERT_FF = 512

N_EVEN = (DEPTH + 1) // 2
N_ODD = DEPTH // 2


def rms_norm(x, g):
    xf = x.astype(jnp.float32)
    y = xf * lax.rsqrt(jnp.mean(xf * xf, axis=-1, keepdims=True) + RMS_EPS)
    return (y * g.astype(jnp.float32)).astype(x.dtype)


def modulate(h, shift, scale):
    return h * (1.0 + scale) + shift


def grid_rope_tables(rows):
    row = jnp.repeat(jnp.arange(rows), GRID_W)
    col = jnp.tile(jnp.arange(GRID_W), rows)
    inv_freq = ROPE_BASE ** (-jnp.arange(ROPE_PAIRS, dtype=jnp.float32) / ROPE_PAIRS)
    ang = jnp.stack([row, col], axis=-1).astype(jnp.float32)[:, :, None] * inv_freq
    return jnp.cos(ang), jnp.sin(ang)


def apply_rope_2d(x, cos, sin):
    xr = x.astype(jnp.float32).reshape(*x.shape[:-1], 2, 2, ROPE_PAIRS)
    xa, xb = xr[..., 0, :], xr[..., 1, :]
    out = jnp.stack([xa * cos - xb * sin, xa * sin + xb * cos], axis=-2)
    return out.reshape(x.shape).astype(x.dtype)


def attend(q, k, v):
    s = jnp.einsum('bqhd,bkhd->bhqk', q, k).astype(jnp.float32) * (MLA_QK ** -0.5)
    p = jax.nn.softmax(s, axis=-1).astype(v.dtype)
    return jnp.einsum('bhqk,bkhd->bqhd', p, v)


def blocked_attend(q, k, v):
    bsz, n, heads, dk = q.shape
    nb = n // Q_BLOCK
    qb = q.reshape(bsz, nb, Q_BLOCK, heads, dk).transpose(1, 0, 2, 3, 4)
    ob = lax.map(lambda qi: attend(qi, k, v), qb)
    return ob.transpose(1, 0, 2, 3, 4).reshape(bsz, n, heads, v.shape[-1])


def depthwise_conv(x, w, b, pad_left, pad_right):
    y = lax.conv_general_dilated(x, w[:, None, :].astype(x.dtype), window_strides=(1,),
                                 padding=[(pad_left, pad_right)],
                                 dimension_numbers=('NWC', 'WIO', 'NWC'),
                                 feature_group_count=x.shape[-1])
    return y + b.astype(x.dtype)


def linear_scan(a, b, h0):
    def combine(left, right):
        a_l, b_l = left
        a_r, b_r = right
        return a_l * a_r, a_r * b_l + b_r
    a_cum, b_cum = lax.associative_scan(combine, (a, b), axis=1)
    return a_cum * h0[:, None, :] + b_cum


def rglru_coeffs(u, w_a, b_a, w_x, b_x, lam):
    bsz, n, width = u.shape
    ub = u.reshape(bsz, n, LRU_BLOCKS, LRU_BLOCK_DIM)
    gate_a = jnp.einsum('bnhi,hij->bnhj', ub, w_a).reshape(bsz, n, width) + b_a
    gate_x = jnp.einsum('bnhi,hij->bnhj', ub, w_x).reshape(bsz, n, width) + b_x
    r = jax.nn.sigmoid(gate_a.astype(jnp.float32))
    i = jax.nn.sigmoid(gate_x.astype(jnp.float32))
    log_a = -LRU_C * r * jax.nn.softplus(-lam.astype(jnp.float32))
    a = jnp.exp(log_a)
    b = jnp.sqrt(-jnp.expm1(2.0 * log_a)) * (i * u.astype(jnp.float32))
    return a, b


def maybe_flip(t, reverse):
    return jnp.flip(t, axis=1) if reverse else t


def rglru_bidirectional(u_ctx, u_lat, w_a, b_a, w_x, b_x, lam):
    zero = jnp.zeros((u_lat.shape[0], u_lat.shape[-1]), jnp.float32)
    outs = []
    for d in range(2):
        rev = d == 1
        a_c, b_c = rglru_coeffs(maybe_flip(u_ctx, rev), w_a[d], b_a[d], w_x[d], b_x[d], lam[d])
        h_c = linear_scan(a_c, b_c, zero)
        a_l, b_l = rglru_coeffs(maybe_flip(u_lat, rev), w_a[d], b_a[d], w_x[d], b_x[d], lam[d])
        h_l = linear_scan(a_l, b_l, h_c[:, -1])
        outs.append((maybe_flip(h_c, rev), maybe_flip(h_l, rev)))
    return outs[0][0] + outs[1][0], outs[0][1] + outs[1][1]


def mla_up(cq, ckv, q_norm_g, kv_norm_g, w_uq, w_ukv):
    lead = cq.shape[:-1]
    q = (rms_norm(cq, q_norm_g) @ w_uq).reshape(*lead, MLA_HEADS, MLA_QK)
    kv = (rms_norm(ckv, kv_norm_g) @ w_ukv).reshape(*lead, MLA_HEADS, MLA_NOPE + MLA_V)
    return q, kv[..., :MLA_NOPE], kv[..., MLA_NOPE:]


def mla_rglru_mixer(h_ctx, h_lat, cos, sin, w_in, q_norm_g, kv_norm_g, w_uq, w_ukv,
                    conv_w, conv_b, w_a, b_a, w_x, b_x, lam, w_out, need_ctx):
    bsz, n_lat, _ = h_lat.shape
    n_ctx = h_ctx.shape[1]
    cq_c, ckv_c, kr_c, ux_c, ug_c = jnp.split(h_ctx @ w_in, EVEN_SPLITS, axis=-1)
    cq_l, ckv_l, kr_l, ux_l, ug_l = jnp.split(h_lat @ w_in, EVEN_SPLITS, axis=-1)

    q_c, kn_c, v_c = mla_up(cq_c, ckv_c, q_norm_g, kv_norm_g, w_uq, w_ukv)
    q_l, kn_l, v_l = mla_up(cq_l, ckv_l, q_norm_g, kv_norm_g, w_uq, w_ukv)
    k_c = jnp.concatenate([kn_c, jnp.broadcast_to(kr_c[:, :, None, :], (bsz, n_ctx, MLA_HEADS, MLA_ROPE))], axis=-1)
    kr_l = apply_rope_2d(kr_l, cos, sin)
    k_l = jnp.concatenate([kn_l, jnp.broadcast_to(kr_l[:, :, None, :], (bsz, n_lat, MLA_HEADS, MLA_ROPE))], axis=-1)
    q_l = jnp.concatenate([q_l[..., :MLA_NOPE], apply_rope_2d(q_l[..., MLA_NOPE:], cos[:, None], sin[:, None])], axis=-1)
    k_all = jnp.concatenate([k_c, k_l], axis=1)
    v_all = jnp.concatenate([v_c, v_l], axis=1)
    att_l = blocked_attend(q_l, k_all, v_all).reshape(bsz, n_lat, MLA_HEADS * MLA_V)

    u_c = depthwise_conv(ux_c, conv_w, conv_b, 2, 1)
    u_l = depthwise_conv(ux_l, conv_w, conv_b, 2, 1)
    y_c, y_l = rglru_bidirectional(u_c, u_l, w_a, b_a, w_x, b_x, lam)
    lru_l = (y_l * jax.nn.gelu(ug_l.astype(jnp.float32))).astype(h_lat.dtype)
    out_l = jnp.concatenate([att_l, lru_l], axis=-1) @ w_out
    if not need_ctx:
        return None, out_l
    att_c = attend(q_c, k_c, v_c).reshape(bsz, n_ctx, MLA_HEADS * MLA_V)
    lru_c = (y_c * jax.nn.gelu(ug_c.astype(jnp.float32))).astype(h_ctx.dtype)
    out_c = jnp.concatenate([att_c, lru_c], axis=-1) @ w_out
    return out_c, out_l


def hyena_filters(n, w1, b1, f1, w2, b2, f2, w3, decay):
    t = jnp.linspace(0.0, 1.0, n, dtype=jnp.float32)[:, None]
    w = (2.0 * math.pi / n) * jnp.arange(n, dtype=jnp.float32)[:, None]
    f = jnp.linspace(1e-4, HY_BANDS - 1, HY_BANDS, dtype=jnp.float32)[None, :]
    z = jnp.concatenate([t, jnp.cos(f * w), -jnp.sin(f * w)], axis=-1)
    a = jnp.sin(f1.astype(jnp.float32) * (z @ w1.astype(jnp.float32) + b1.astype(jnp.float32)))
    a = jnp.sin(f2.astype(jnp.float32) * (a @ w2.astype(jnp.float32) + b2.astype(jnp.float32)))
    h = (a @ w3.astype(jnp.float32)).reshape(n, 2 * HY_ORDER, HY_WIDTH)
    h = h * jnp.exp(-t[:, :, None] * jnp.abs(decay.astype(jnp.float32)))
    h = h.reshape(n, HY_ORDER, 2, HY_WIDTH)
    return h / (jnp.sum(jnp.abs(h), axis=(0, 2), keepdims=True) + 1e-6)


def two_sided_fftconv(u, h_fwd, h_bwd, skip):
    n = u.shape[1]
    u32 = u.astype(jnp.float32)
    spec_u = jnp.fft.rfft(u32, n=2 * n, axis=1)
    spec_h = jnp.fft.rfft(h_fwd, n=2 * n, axis=0) + jnp.conj(jnp.fft.rfft(h_bwd, n=2 * n, axis=0))
    y = jnp.fft.irfft(spec_u * spec_h[None], n=2 * n, axis=1)[:, :n]
    return (y + u32 * skip.astype(jnp.float32)).astype(u.dtype)


def hyena_operator(h, w_in, conv_w, conv_b, w1, b1, f1, w2, b2, f2, w3, decay, skip, w_out):
    n = h.shape[1]
    p = depthwise_conv(h @ w_in, conv_w, conv_b, 1, 1)
    v, x1, x2 = jnp.split(p, 3, axis=-1)
    filt = hyena_filters(n, w1, b1, f1, w2, b2, f2, w3, decay)
    z = v
    for order, gate in enumerate((x1, x2)):
        z = gate * two_sided_fftconv(z, filt[:, order, 0], filt[:, order, 1], skip[order])
    return z @ w_out


def grouped_moe(h, router_w, router_b, w1, w3, w2):
    s = jax.nn.sigmoid(jnp.einsum('bld,de->ble', h, router_w).astype(jnp.float32))
    sel = s + router_b.astype(jnp.float32)
    sel_g = sel.reshape(*sel.shape[:-1], N_GROUPS, EXPERTS_PER_GROUP)
    group_score = jnp.sum(lax.top_k(sel_g, 2)[0], axis=-1)
    g_idx = jnp.argmax(group_score, axis=-1)
    in_group = (jnp.arange(N_EXPERTS) // EXPERTS_PER_GROUP) == g_idx[..., None]
    _, e_idx = lax.top_k(jnp.where(in_group, sel, -jnp.inf), TOP_K)
    w_sel = jnp.take_along_axis(s, e_idx, axis=-1)
    w_sel = w_sel / jnp.sum(w_sel, axis=-1, keepdims=True)
    gates = jnp.sum(jax.nn.one_hot(e_idx, N_EXPERTS, dtype=jnp.float32) * w_sel[..., None], axis=-2).astype(h.dtype)
    y = jnp.zeros_like(h)
    for e in range(N_EXPERTS):
        act = jax.nn.silu(h @ w1[e]) * (h @ w3[e])
        y = y + gates[..., e:e + 1] * (act @ w2[e])
    return y


def _forward_jax(x, c, ctx, c_ctx, ada_w, ada_b, norm1_g, norm2_g, final_g,
                 ev_w_in, mla_q_norm_g, mla_kv_norm_g, mla_w_uq, mla_w_ukv,
                 lru_conv_w, lru_conv_b, lru_w_a, lru_b_a, lru_w_x, lru_b_x, lru_lambda, ev_w_out,
                 od_w_in, hy_conv_w, hy_conv_b, hy_w1, hy_b1, hy_freq1, hy_w2, hy_b2, hy_freq2,
                 hy_w3, hy_decay, hy_skip, od_w_out,
                 router_w, router_b, moe_w1, moe_w3, moe_w2):
    n_lat = x.shape[1]
    n_ctx = ctx.shape[1]
    rows = n_lat // GRID_W
    cos, sin = grid_rope_tables(rows)
    silu_c = jax.nn.silu(c)
    silu_cc = jax.nn.silu(c_ctx)
    xc = ctx
    for layer in range(DEPTH):
        need_ctx = layer < DEPTH - 1
        is_even = layer % 2 == 0
        idx = layer // 2
        mod = (silu_c @ ada_w[layer] + ada_b[layer])[:, None, :]
        sh1, sc1, g1, sh2, sc2, g2 = jnp.split(mod, 6, axis=-1)
        h_lat = modulate(rms_norm(x, norm1_g[layer]), sh1, sc1)
        h_ctx = None
        if need_ctx or is_even:
            mod_c = silu_cc @ ada_w[layer] + ada_b[layer]
            csh1, csc1, cg1, csh2, csc2, cg2 = jnp.split(mod_c, 6, axis=-1)
            h_ctx = modulate(rms_norm(xc, norm1_g[layer]), csh1, csc1)
        if is_even:
            o_ctx, o_lat = mla_rglru_mixer(h_ctx, h_lat, cos, sin, ev_w_in[idx], mla_q_norm_g[idx],
                                           mla_kv_norm_g[idx], mla_w_uq[idx], mla_w_ukv[idx],
                                           lru_conv_w[idx], lru_conv_b[idx], lru_w_a[idx], lru_b_a[idx],
                                           lru_w_x[idx], lru_b_x[idx], lru_lambda[idx], ev_w_out[idx],
                                           need_ctx)
        else:
            hy_params = (od_w_in[idx], hy_conv_w[idx], hy_conv_b[idx], hy_w1[idx], hy_b1[idx],
                         hy_freq1[idx], hy_w2[idx], hy_b2[idx], hy_freq2[idx], hy_w3[idx],
                         hy_decay[idx], hy_skip[idx], od_w_out[idx])
            o_lat = hyena_operator(h_lat, *hy_params)
            o_ctx = hyena_operator(h_ctx, *hy_params) if need_ctx else None
        x = x + g1 * o_lat
        h_lat = modulate(rms_norm(x, norm2_g[layer]), sh2, sc2)
        if need_ctx:
            xc = xc + cg1 * o_ctx
            h_ctx = modulate(rms_norm(xc, norm2_g[layer]), csh2, csc2)
            m = grouped_moe(jnp.concatenate([h_ctx, h_lat], axis=1), router_w, router_b,
                            moe_w1[layer], moe_w3[layer], moe_w2[layer])
            xc = xc + cg2 * m[:, :n_ctx]
            m_lat = m[:, n_ctx:]
        else:
            m_lat = grouped_moe(h_lat, router_w, router_b, moe_w1[layer], moe_w3[layer], moe_w2[layer])
        x = x + g2 * m_lat
    return _final_norm(x, final_g)


def _final_norm_kernel(x_ref, g_ref, o_ref):
    xf = x_ref[...]
    y = xf * lax.rsqrt(jnp.mean(xf * xf, axis=-1, keepdims=True) + RMS_EPS)
    o_ref[...] = y * g_ref[...]


def _final_norm(x, g):
    b, n, d = x.shape
    x2 = x.reshape(b * n, d)
    tm = 1024
    out = pl.pallas_call(
        _final_norm_kernel,
        grid=(b * n // tm,),
        in_specs=[pl.BlockSpec((tm, d), lambda i: (i, 0)), pl.BlockSpec((1, d), lambda i: (0, 0))],
        out_specs=pl.BlockSpec((tm, d), lambda i: (i, 0)),
        out_shape=jax.ShapeDtypeStruct((b * n, d), x.dtype),
    )(x2, g.reshape(1, d))
    return out.reshape(b, n, d)


def kernel(x, c, ctx, c_ctx, ada_w, ada_b, norm1_g, norm2_g, final_g, ev_w_in, mla_q_norm_g, mla_kv_norm_g, mla_w_uq, mla_w_ukv, lru_conv_w, lru_conv_b, lru_w_a, lru_b_a, lru_w_x, lru_b_x, lru_lambda, ev_w_out, od_w_in, hy_conv_w, hy_conv_b, hy_w1, hy_b1, hy_freq1, hy_w2, hy_b2, hy_freq2, hy_w3, hy_decay, hy_skip, od_w_out, router_w, router_b, moe_w1, moe_w3, moe_w2):
    return _forward_jax(x, c, ctx, c_ctx, ada_w, ada_b, norm1_g, norm2_g, final_g, ev_w_in, mla_q_norm_g, mla_kv_norm_g, mla_w_uq, mla_w_ukv, lru_conv_w, lru_conv_b, lru_w_a, lru_b_a, lru_w_x, lru_b_x, lru_lambda, ev_w_out, od_w_in, hy_conv_w, hy_conv_b, hy_w1, hy_b1, hy_freq1, hy_w2, hy_b2, hy_freq2, hy_w3, hy_decay, hy_skip, od_w_out, router_w, router_b, moe_w1, moe_w3, moe_w2)
```

```python
import functools
import math

import jax
import jax.numpy as jnp
from jax import lax
from jax.experimental import pallas as pl
from jax.experimental.pallas import tpu as pltpu

D_MODEL = 1024
DEPTH = 2
GRID_W = 64
RMS_EPS = 1e-6

MLA_HEADS = 8
MLA_Q_LORA = 384
MLA_KV_LORA = 256
MLA_NOPE = 64
MLA_ROPE = 32
MLA_V = 64
MLA_QK = MLA_NOPE + MLA_ROPE
ROPE_PAIRS = MLA_ROPE // 4
ROPE_BASE = 10000.0

LRU_WIDTH = 512
LRU_BLOCKS = 8
LRU_BLOCK_DIM = LRU_WIDTH // LRU_BLOCKS
LRU_C = 8.0
LRU_CONV = 4

HY_WIDTH = D_MODEL
HY_ORDER = 2
HY_BANDS = 16
HY_HIDDEN = 64

N_EXPERTS = 16
N_GROUPS = 4
EXPERTS_PER_GROUP = N_EXPERTS // N_GROUPS

LANES = 128
SUBLANES = 8
HEAD_PAD = LANES

TOKEN_TILE = 256
MOE_ROW_TILE = 512
ATTN_Q_TILE = 512
LRU_CHUNK = 256

BF16 = jnp.bfloat16
F32 = jnp.float32
_HI = lax.Precision.HIGHEST
_NEG_INF = float('-inf')


def _dot(a, b):
    return jnp.dot(a, b, preferred_element_type=F32)


def _rms(x, g):
    return x * lax.rsqrt(jnp.mean(x * x, axis=-1, keepdims=True) + RMS_EPS) * g


def _mod_map(nb, nt, n_ctx_tiles):
    if n_ctx_tiles:
        return lambda b, j: (jnp.where(j >= nt - n_ctx_tiles, nb, b), 0, 0)
    return lambda b, j: (b, 0, 0)


_SEQ2 = pltpu.CompilerParams(dimension_semantics=("arbitrary", "arbitrary"))


def _rows_matmul_kernel(x_ref, w_ref, b_ref, o_ref):
    o_ref[...] = _dot(x_ref[...].astype(BF16), w_ref[...].astype(BF16)) + b_ref[...]


def rows_matmul(x, w, b, *, tn):
    m, k = x.shape
    n = w.shape[1]
    return pl.pallas_call(
        _rows_matmul_kernel,
        grid=(n // tn,),
        in_specs=[pl.BlockSpec((m, k), lambda j: (0, 0)),
                  pl.BlockSpec((k, tn), lambda j: (0, j)),
                  pl.BlockSpec((1, tn), lambda j: (0, j))],
        out_specs=pl.BlockSpec((m, tn), lambda j: (0, j)),
        out_shape=jax.ShapeDtypeStruct((m, n), F32),
        name="rows_matmul",
    )(x, w, b.reshape(1, n))


def _rot_cols(w):
    p = ROPE_PAIRS
    return jnp.concatenate([-w[:, p:2 * p], w[:, 0:p], -w[:, 3 * p:4 * p], w[:, 2 * p:3 * p]], axis=1)


def _head_pad_cols(w, width):
    k = w.shape[0]
    w = w.reshape(k, MLA_HEADS, width)
    return jnp.pad(w, ((0, 0), (0, 0), (0, HEAD_PAD - width))).reshape(k, MLA_HEADS * HEAD_PAD)


def prep_even_weights(w_in, w_uq, w_ukv, w_out):
    d = w_in.shape[0]
    s1 = MLA_Q_LORA + MLA_KV_LORA
    s2 = s1 + MLA_ROPE
    w_kr = w_in[:, s1:s2]
    place = lambda w: jnp.pad(w, ((0, 0), (MLA_NOPE, HEAD_PAD - MLA_QK)))
    w_big = jnp.concatenate([w_in[:, :s1], w_in[:, s2:], place(w_kr), place(_rot_cols(w_kr))], axis=1).astype(BF16)
    uq = w_uq.reshape(MLA_Q_LORA, MLA_HEADS, MLA_QK)
    wq_a = _head_pad_cols(w_uq, MLA_QK)
    uq_rot = jnp.stack([_rot_cols(uq[:, h, MLA_NOPE:]) for h in range(MLA_HEADS)], axis=1)
    wq_b = jnp.pad(uq_rot, ((0, 0), (0, 0), (MLA_NOPE, HEAD_PAD - MLA_QK))).reshape(MLA_Q_LORA, MLA_HEADS * HEAD_PAD)
    ukv = w_ukv.reshape(MLA_KV_LORA, MLA_HEADS, MLA_NOPE + MLA_V)
    wk = _head_pad_cols(ukv[:, :, :MLA_NOPE].reshape(MLA_KV_LORA, -1), MLA_NOPE)
    wv = _head_pad_cols(ukv[:, :, MLA_NOPE:].reshape(MLA_KV_LORA, -1), MLA_V)
    w_kv = jnp.concatenate([wk, wv], axis=1)
    att_rows = MLA_HEADS * MLA_V
    wo_att = jnp.pad(w_out[:att_rows].reshape(MLA_HEADS, MLA_V, d),
                     ((0, 0), (0, HEAD_PAD - MLA_V), (0, 0))).reshape(MLA_HEADS * HEAD_PAD, d)
    return dict(w_big=w_big, wq_a=wq_a.astype(BF16), wq_b=wq_b.astype(BF16), w_kv=w_kv.astype(BF16),
                wo_att=wo_att.astype(BF16), wo_lru=w_out[att_rows:].astype(BF16))


def rope_tables(n_lat, n_ctx):
    rows = n_lat // GRID_W
    row = jnp.repeat(jnp.arange(rows), GRID_W)
    col = jnp.tile(jnp.arange(GRID_W), rows)
    inv_freq = ROPE_BASE ** (-jnp.arange(ROPE_PAIRS, dtype=F32) / ROPE_PAIRS)
    ang = jnp.stack([row, col], axis=-1).astype(F32)[:, :, None] * inv_freq
    cos, sin = jnp.cos(ang), jnp.sin(ang)
    c32 = jnp.concatenate([cos[:, 0], cos[:, 0], cos[:, 1], cos[:, 1]], axis=-1)
    s32 = jnp.concatenate([sin[:, 0], sin[:, 0], sin[:, 1], sin[:, 1]], axis=-1)
    c32 = jnp.concatenate([c32, jnp.ones((n_ctx, MLA_ROPE), F32)], axis=0)
    s32 = jnp.concatenate([s32, jnp.zeros((n_ctx, MLA_ROPE), F32)], axis=0)
    n = n_lat + n_ctx
    pad_hi = jnp.zeros((n, HEAD_PAD - MLA_QK), F32)
    scale = MLA_QK ** -0.5
    cq = jnp.concatenate([jnp.full((n, MLA_NOPE), scale, F32), c32 * scale, pad_hi], axis=-1)
    sq = jnp.concatenate([jnp.zeros((n, MLA_NOPE), F32), s32 * scale, pad_hi], axis=-1)
    ck = jnp.concatenate([jnp.zeros((n, MLA_NOPE), F32), c32, pad_hi], axis=-1)
    sk = jnp.concatenate([jnp.zeros((n, MLA_NOPE), F32), s32, pad_hi], axis=-1)
    return cq, sq, ck, sk


def _even_in_kernel(x_ref, sh_ref, sc_ref, g_ref, wbig_ref, qg_ref, kvg_ref, wqa_ref, wqb_ref, wkv_ref,
                    cq_ref, sq_ref, ck_ref, sk_ref, q_ref, k_ref, v_ref, ux_ref, ug_ref):
    h = (_rms(x_ref[...], g_ref[...]) * (1.0 + sc_ref[...]) + sh_ref[...]).astype(BF16)
    p = _dot(h, wbig_ref[...])
    o1 = MLA_Q_LORA
    o2 = o1 + MLA_KV_LORA
    o3 = o2 + LRU_WIDTH
    o4 = o3 + LRU_WIDTH
    o5 = o4 + HEAD_PAD
    ux_ref[...] = p[:, o2:o3]
    ug_ref[...] = p[:, o3:o4]
    cqn = _rms(p[:, :o1], qg_ref[...]).astype(BF16)
    rep = lambda t: jnp.concatenate([t] * MLA_HEADS, axis=-1)
    q = _dot(cqn, wqa_ref[...]) * rep(cq_ref[...]) + _dot(cqn, wqb_ref[...]) * rep(sq_ref[...])
    q_ref[...] = q.astype(BF16)
    ckvn = _rms(p[:, o1:o2], kvg_ref[...]).astype(BF16)
    kv = _dot(ckvn, wkv_ref[...])
    k_rope = p[:, o4:o5] * ck_ref[...] + p[:, o5:] * sk_ref[...]
    hw = MLA_HEADS * HEAD_PAD
    k_ref[...] = (kv[:, :hw] + rep(k_rope)).astype(BF16)
    v_ref[...] = kv[:, hw:].astype(BF16)


def even_in_proj(x, shift, scale, norm_g, wts, q_norm_g, kv_norm_g, tables, *, tm, n_ctx_tiles):
    nb, s_len, d = x.shape
    nt = s_len // tm
    hw = MLA_HEADS * HEAD_PAD
    mod_map = _mod_map(nb, nt, n_ctx_tiles)
    const = lambda b, j: (0, 0)
    tok = lambda b, j: (b, j, 0)
    tab = lambda b, j: (j, 0)
    full = lambda a: pl.BlockSpec(a.shape, const)
    row = lambda v: v.reshape(1, -1)
    args = [x, shift, scale, row(norm_g), wts['w_big'], row(q_norm_g), row(kv_norm_g),
            wts['wq_a'], wts['wq_b'], wts['w_kv']]
    in_specs = [pl.BlockSpec((None, tm, d), tok), pl.BlockSpec((None, 1, d), mod_map),
                pl.BlockSpec((None, 1, d), mod_map)]
    in_specs += [full(a) for a in args[3:]]
    in_specs += [pl.BlockSpec((tm, HEAD_PAD), tab)] * 4
    out_dims = [(hw, BF16), (hw, BF16), (hw, BF16), (LRU_WIDTH, F32), (LRU_WIDTH, F32)]
    return pl.pallas_call(
        _even_in_kernel,
        grid=(nb, nt),
        in_specs=in_specs,
        out_specs=[pl.BlockSpec((None, tm, w), tok) for w, _ in out_dims],
        out_shape=[jax.ShapeDtypeStruct((nb, s_len, w), dt) for w, dt in out_dims],
        compiler_params=_SEQ2,
        name="even_in_proj",
    )(*args, *tables)


def _attn_kernel(q_ref, k_ref, v_ref, o_ref):
    s = lax.dot_general(q_ref[...], k_ref[...], (((1,), (1,)), ((), ())), preferred_element_type=F32)
    m = jnp.max(s, axis=-1, keepdims=True)
    p = jnp.exp(s - m)
    l = jnp.sum(p, axis=-1, keepdims=True)
    o = _dot(p.astype(BF16), v_ref[...])
    o_ref[...] = (o / l).astype(o_ref.dtype)


def attention(q, k, v, *, q_start, n_q, k_start, n_k, tq):
    nb = q.shape[0]
    qo = q_start // tq
    ko = k_start // n_k
    return pl.pallas_call(
        _attn_kernel,
        grid=(nb, MLA_HEADS, n_q // tq),
        in_specs=[pl.BlockSpec((None, tq, HEAD_PAD), lambda b, h, i: (b, qo + i, h)),
                  pl.BlockSpec((None, n_k, HEAD_PAD), lambda b, h, i: (b, ko, h)),
                  pl.BlockSpec((None, n_k, HEAD_PAD), lambda b, h, i: (b, ko, h))],
        out_specs=pl.BlockSpec((None, tq, HEAD_PAD), lambda b, h, i: (b, i, h)),
        out_shape=jax.ShapeDtypeStruct((nb, n_q, MLA_HEADS * HEAD_PAD), BF16),
        compiler_params=pltpu.CompilerParams(dimension_semantics=("arbitrary", "arbitrary", "arbitrary")),
        name="attention",
    )(q, k, v)


def _scan_block(a, b, h_prev, row, reverse):
    for s in (1, 2, 4):
        sh = SUBLANES - s if reverse else s
        a_s = pltpu.roll(a, sh, 0)
        b_s = pltpu.roll(b, sh, 0)
        ok = (row < SUBLANES - s) if reverse else (row >= s)
        b = jnp.where(ok, a * b_s + b, b)
        a = jnp.where(ok, a * a_s, a)
    return a * h_prev + b


def _rglru_kernel(ux_ref, ug_ref, cw_ref, cb_ref, wa_ref, ba_ref, wx_ref, bx_ref, lam_ref, o_ref,
                  xp_ref, a_ref, b_ref, hf_ref, hb_ref, *, n_lat, n_ctx):
    lanes = ux_ref.shape[-1]
    pad = SUBLANES
    zeros = jnp.zeros((pad, lanes), F32)
    lat0 = pad
    ctx0 = 2 * pad + n_lat
    xp_ref[0:pad, :] = zeros
    xp_ref[lat0 + n_lat:ctx0, :] = zeros
    xp_ref[ctx0 + n_ctx:ctx0 + n_ctx + pad, :] = zeros
    xp_ref[lat0:lat0 + n_lat, :] = ux_ref[0:n_lat, :]
    xp_ref[ctx0:ctx0 + n_ctx, :] = ux_ref[n_lat:n_lat + n_ctx, :]
    cw = cw_ref[...]
    n_chunks = (n_lat + n_ctx) // LRU_CHUNK
    n_lat_chunks = n_lat // LRU_CHUNK
    n_win = LRU_CHUNK + 2 * pad

    def coeffs(c, carry):
        src = pl.multiple_of(c * LRU_CHUNK + jnp.where(c >= n_lat_chunks, pad, 0), SUBLANES)
        dst = pl.multiple_of(c * LRU_CHUNK, SUBLANES)
        win = xp_ref[pl.ds(src, n_win), :]
        mid = lambda t: t[pad:pad + LRU_CHUNK]
        u = cb_ref[...] + cw[2:3, :] * mid(win)
        u = u + cw[0:1, :] * mid(pltpu.roll(win, 2, 0))
        u = u + cw[1:2, :] * mid(pltpu.roll(win, 1, 0))
        u = u + cw[3:4, :] * mid(pltpu.roll(win, n_win - 1, 0))
        ub = u.astype(BF16)
        for d in range(2):
            r = jax.nn.sigmoid(_dot(ub, wa_ref[d]) + ba_ref[d])
            i = jax.nn.sigmoid(_dot(ub, wx_ref[d]) + bx_ref[d])
            a = jnp.exp(lam_ref[d] * r)
            a_ref[d, pl.ds(dst, LRU_CHUNK), :] = a
            b_ref[d, pl.ds(dst, LRU_CHUNK), :] = jnp.sqrt(1.0 - a * a) * (i * u)
        return carry

    lax.fori_loop(0, n_chunks, coeffs, 0)
    row = lax.broadcasted_iota(jnp.int32, (SUBLANES, lanes), 0)

    def make_step(first_blk, n_blk):
        def step(t, carry):
            h_f, h_b = carry
            rf = pl.multiple_of((first_blk + t) * SUBLANES, SUBLANES)
            rb = pl.multiple_of((first_blk + n_blk - 1 - t) * SUBLANES, SUBLANES)
            out_f = _scan_block(a_ref[0, pl.ds(rf, SUBLANES), :], b_ref[0, pl.ds(rf, SUBLANES), :], h_f, row, False)
            out_b = _scan_block(a_ref[1, pl.ds(rb, SUBLANES), :], b_ref[1, pl.ds(rb, SUBLANES), :], h_b, row, True)
            hf_ref[pl.ds(rf, SUBLANES), :] = out_f
            hb_ref[pl.ds(rb, SUBLANES), :] = out_b
            h_f = jnp.broadcast_to(out_f[SUBLANES - 1:SUBLANES, :], out_f.shape)
            h_b = jnp.broadcast_to(out_b[0:1, :], out_b.shape)
            return h_f, h_b
        return step

    state = (jnp.zeros((SUBLANES, lanes), F32), jnp.zeros((SUBLANES, lanes), F32))
    state = lax.fori_loop(0, n_ctx // SUBLANES, make_step(n_lat // SUBLANES, n_ctx // SUBLANES), state)
    lax.fori_loop(0, n_lat // SUBLANES, make_step(0, n_lat // SUBLANES), state)

    def gate(c, carry):
        r0 = pl.multiple_of(c * LRU_CHUNK, SUBLANES)
        g = ug_ref[pl.ds(r0, LRU_CHUNK), :]
        gelu = 0.5 * g * (1.0 + jnp.tanh(math.sqrt(2.0 / math.pi) * (g + 0.044715 * (g * g * g))))
        y = hf_ref[pl.ds(r0, LRU_CHUNK), :] + hb_ref[pl.ds(r0, LRU_CHUNK), :]
        o_ref[pl.ds(r0, LRU_CHUNK), :] = (y * gelu).astype(o_ref.dtype)
        return carry

    lax.fori_loop(0, n_chunks, gate, 0)


def _block_diag(w):
    per = LANES // LRU_BLOCK_DIM
    nt = LRU_BLOCKS // per
    w = w.reshape(2, nt, per, LRU_BLOCK_DIM, LRU_BLOCK_DIM)
    eye = jnp.eye(per, dtype=w.dtype)
    out = jnp.einsum('dtpij,pq->dtpiqj', w, eye)
    return out.reshape(2, nt, LANES, LANES)


def rglru(ux, ug, conv_w, conv_b, w_a, b_a, w_x, b_x, lam, *, n_lat, n_ctx):
    nb, s_len, width = ux.shape
    nt = width // LANES
    wa = _block_diag(w_a).astype(BF16).transpose(1, 0, 2, 3)
    wx = _block_diag(w_x).astype(BF16).transpose(1, 0, 2, 3)
    lam_c = -LRU_C * jax.nn.softplus(-lam.astype(F32))
    lane3 = lambda v: v.reshape(2, 1, width)
    seq = pl.BlockSpec((None, s_len, LANES), lambda b, c: (b, 0, c))
    vec = lambda rows: pl.BlockSpec((rows, LANES), lambda b, c: (0, c))
    dvec = pl.BlockSpec((2, 1, LANES), lambda b, c: (0, 0, c))
    wspec = pl.BlockSpec((None, 2, LANES, LANES), lambda b, c: (c, 0, 0, 0))
    return pl.pallas_call(
        functools.partial(_rglru_kernel, n_lat=n_lat, n_ctx=n_ctx),
        grid=(nb, nt),
        in_specs=[seq, seq, vec(LRU_CONV), vec(1), wspec, dvec, wspec, dvec, dvec],
        out_specs=seq,
        out_shape=jax.ShapeDtypeStruct((nb, s_len, width), BF16),
        scratch_shapes=[pltpu.VMEM((s_len + 3 * SUBLANES, LANES), F32),
                        pltpu.VMEM((2, s_len, LANES), F32), pltpu.VMEM((2, s_len, LANES), F32),
                        pltpu.VMEM((s_len, LANES), F32), pltpu.VMEM((s_len, LANES), F32)],
        compiler_params=_SEQ2,
        name="rglru",
    )(ux, ug, conv_w, conv_b.reshape(1, width), wa, lane3(b_a), wx, lane3(b_x), lane3(lam_c))


def _even_out_kernel(x_ref, gate_ref, att_ref, lru_ref, wa_ref, wl_ref, o_ref):
    y = _dot(att_ref[...], wa_ref[...]) + _dot(lru_ref[...], wl_ref[...])
    o_ref[...] = x_ref[...] + gate_ref[...] * y


def even_out_proj(x, gate, att, lru, wo_att, wo_lru, *, tm, n_ctx_tiles):
    nb, s_len, d = x.shape
    nt = s_len // tm
    tok = lambda b, j: (b, j, 0)
    const = lambda b, j: (0, 0)
    return pl.pallas_call(
        _even_out_kernel,
        grid=(nb, nt),
        in_specs=[pl.BlockSpec((None, tm, d), tok), pl.BlockSpec((None, 1, d), _mod_map(nb, nt, n_ctx_tiles)),
                  pl.BlockSpec((None, tm, att.shape[-1]), tok), pl.BlockSpec((None, tm, lru.shape[-1]), tok),
                  pl.BlockSpec(wo_att.shape, const), pl.BlockSpec(wo_lru.shape, const)],
        out_specs=pl.BlockSpec((None, tm, d), tok),
        out_shape=jax.ShapeDtypeStruct((nb, s_len, d), F32),
        compiler_params=_SEQ2,
        name="even_out_proj",
    )(x, gate, att, lru, wo_att, wo_lru)


def even_mixer(x_all, shift, scale, gate, norm_g, p, *, n_lat, n_ctx):
    n_ctx_tiles = n_ctx // TOKEN_TILE
    tables = rope_tables(n_lat, n_ctx)
    wts = prep_even_weights(p['w_in'], p['w_uq'], p['w_ukv'], p['w_out'])
    q, k, v, ux, ug = even_in_proj(x_all, shift, scale, norm_g, wts, p['q_norm_g'], p['kv_norm_g'], tables,
                                   tm=TOKEN_TILE, n_ctx_tiles=n_ctx_tiles)
    att_l = attention(q, k, v, q_start=0, n_q=n_lat, k_start=0, n_k=n_lat + n_ctx, tq=ATTN_Q_TILE)
    att_c = attention(q, k, v, q_start=n_lat, n_q=n_ctx, k_start=n_lat, n_k=n_ctx, tq=n_ctx)
    att = jnp.concatenate([att_l, att_c], axis=1)
    lru = rglru(ux, ug, p['conv_w'], p['conv_b'], p['w_a'], p['b_a'], p['w_x'], p['b_x'], p['lam'],
                n_lat=n_lat, n_ctx=n_ctx)
    return even_out_proj(x_all, gate, att, lru, wts['wo_att'], wts['wo_lru'], tm=TOKEN_TILE, n_ctx_tiles=n_ctx_tiles)


def _first_argmax(vals):
    best = vals[0]
    idx = jnp.zeros(vals[0].shape, jnp.int32)
    for j in range(1, len(vals)):
        better = vals[j] > best
        idx = jnp.where(better, j, idx)
        best = jnp.where(better, vals[j], best)
    return idx, best


def _select(idx, vals):
    out = vals[0]
    for j in range(1, len(vals)):
        out = jnp.where(idx == j, vals[j], out)
    return out


def _moe_route_kernel(x_ref, sh_ref, sc_ref, g_ref, rwt_ref, rb_ref,
                      h_ref, e_ref, w_ref, rank_ref, cnt_ref, carry_ref, *, n_groups, per_group):
    first = jnp.logical_and(pl.program_id(0) == 0, pl.program_id(1) == 0)

    @pl.when(first)
    def _():
        carry_ref[...] = jnp.zeros_like(carry_ref)

    tm = x_ref.shape[0]
    n_exp = n_groups * per_group
    h = _rms(x_ref[...], g_ref[...]) * (1.0 + sc_ref[...]) + sh_ref[...]
    h_ref[...] = h
    logits = lax.dot_general(rwt_ref[...], h, (((1,), (1,)), ((), ())),
                             precision=_HI, preferred_element_type=F32)
    s = jax.nn.sigmoid(logits)
    sel = s + rb_ref[...]
    sel_rows = [sel[e:e + 1, :] for e in range(n_exp)]
    s_rows = [s[e:e + 1, :] for e in range(n_exp)]
    g_scores = []
    for g in range(n_groups):
        r = sel_rows[g * per_group:(g + 1) * per_group]
        pair_sums = [r[a] + r[b] for a in range(per_group) for b in range(a + 1, per_group)]
        g_scores.append(functools.reduce(jnp.maximum, pair_sums))
    g_idx, _ = _first_argmax(g_scores)
    v = [_select(g_idx, [sel_rows[g * per_group + j] for g in range(n_groups)]) for j in range(per_group)]
    sv = [_select(g_idx, [s_rows[g * per_group + j] for g in range(n_groups)]) for j in range(per_group)]
    i1, _ = _first_argmax(v)
    i2, _ = _first_argmax([jnp.full_like(v[0], _NEG_INF)]
                          + [jnp.where(i1 == j, _NEG_INF, v[j]) for j in range(per_group)])
    i2 = i2 - 1
    w1 = _select(i1, sv)
    w2 = _select(i2, sv)
    wsum = w1 + w2
    e1 = g_idx * per_group + i1
    e2 = g_idx * per_group + i2
    e_ref[0:1, :] = e1
    e_ref[1:2, :] = e2
    w_ref[0:1, :] = w1 / wsum
    w_ref[1:2, :] = w2 / wsum
    e_iota = lax.broadcasted_iota(jnp.int32, (n_exp, tm), 0)
    oh1 = e_iota == e1
    oh2 = e_iota == e2
    m = jnp.where(oh1, 1.0, jnp.where(oh2, 1.0, 0.0))
    r_i = lax.broadcasted_iota(jnp.int32, (tm, tm), 0)
    c_i = lax.broadcasted_iota(jnp.int32, (tm, tm), 1)
    upper = jnp.where(r_i < c_i, 1.0, 0.0).astype(BF16)
    pre = _dot(m.astype(BF16), upper)
    tot = pre + carry_ref[:, 0:1]
    rank_ref[0:1, :] = jnp.sum(jnp.where(oh1, tot, 0.0), axis=0, keepdims=True).astype(jnp.int32)
    rank_ref[1:2, :] = jnp.sum(jnp.where(oh2, tot, 0.0), axis=0, keepdims=True).astype(jnp.int32)
    new_carry = carry_ref[...] + jnp.sum(m, axis=1, keepdims=True)
    carry_ref[...] = new_carry
    cnt_ref[...] = new_carry


def moe_route(x, shift, scale, norm_g, router_w, router_b, *, tm, n_ctx_tiles, n_groups, per_group):
    nb, s_len, d = x.shape
    nt = s_len // tm
    n_exp = n_groups * per_group
    rwt = router_w.T
    rb = router_b.reshape(n_exp, 1)
    mod_map = _mod_map(nb, nt, n_ctx_tiles)
    tok_map = lambda b, j: (b * nt + j, 0, 0)
    return pl.pallas_call(
        functools.partial(_moe_route_kernel, n_groups=n_groups, per_group=per_group),
        grid=(nb, nt),
        in_specs=[
            pl.BlockSpec((None, tm, d), lambda b, j: (b, j, 0)),
            pl.BlockSpec((None, 1, d), mod_map),
            pl.BlockSpec((None, 1, d), mod_map),
            pl.BlockSpec((1, d), lambda b, j: (0, 0)),
            pl.BlockSpec((n_exp, d), lambda b, j: (0, 0)),
            pl.BlockSpec((n_exp, 1), lambda b, j: (0, 0)),
        ],
        out_specs=[
            pl.BlockSpec((None, tm, d), lambda b, j: (b, j, 0)),
            pl.BlockSpec((None, 2, tm), tok_map),
            pl.BlockSpec((None, 2, tm), tok_map),
            pl.BlockSpec((None, 2, tm), tok_map),
            pl.BlockSpec((n_exp, LANES), lambda b, j: (0, 0)),
        ],
        out_shape=[
            jax.ShapeDtypeStruct((nb, s_len, d), F32),
            jax.ShapeDtypeStruct((nb * nt, 2, tm), jnp.int32),
            jax.ShapeDtypeStruct((nb * nt, 2, tm), F32),
            jax.ShapeDtypeStruct((nb * nt, 2, tm), jnp.int32),
            jax.ShapeDtypeStruct((n_exp, LANES), F32),
        ],
        scratch_shapes=[pltpu.VMEM((n_exp, LANES), F32)],
        compiler_params=_SEQ2,
        name="moe_route",
    )(x, shift, scale, norm_g.reshape(1, d), rwt, rb)


def _row_copy(src_ref, src_row, dst_ref, dst_row, sem):
    return pltpu.make_async_copy(src_ref.at[pl.ds(src_row, 1)], dst_ref.at[pl.ds(dst_row, 1)], sem)


def _moe_dispatch_kernel(pos_ref, h_ref, xs_in_ref, xs_ref, sem):
    del xs_in_ref
    tm = h_ref.shape[0]

    def start(r, c):
        _row_copy(h_ref, r, xs_ref, pos_ref[0, r], sem).start()
        _row_copy(h_ref, r, xs_ref, pos_ref[1, r], sem).start()
        return c

    lax.fori_loop(0, tm, start, 0)

    def wait(r, c):
        _row_copy(h_ref, 0, xs_ref, 0, sem).wait()
        _row_copy(h_ref, 0, xs_ref, 0, sem).wait()
        return c

    lax.fori_loop(0, tm, wait, 0)


def moe_dispatch(h2d, pos, n_rows, *, tm):
    t, d = h2d.shape
    zeros = jnp.zeros((n_rows, d), h2d.dtype)
    return pl.pallas_call(
        _moe_dispatch_kernel,
        grid=(t // tm,),
        in_specs=[
            pl.BlockSpec((None, 2, tm), lambda i: (i, 0, 0), memory_space=pltpu.SMEM),
            pl.BlockSpec((tm, d), lambda i: (i, 0)),
            pl.BlockSpec(memory_space=pl.ANY),
        ],
        out_specs=pl.BlockSpec(memory_space=pl.ANY),
        out_shape=jax.ShapeDtypeStruct((n_rows, d), h2d.dtype),
        scratch_shapes=[pltpu.SemaphoreType.DMA(())],
        input_output_aliases={2: 0},
        compiler_params=pltpu.CompilerParams(dimension_semantics=("arbitrary",), has_side_effects=True),
        name="moe_dispatch",
    )(pos, h2d, zeros)


def _moe_expert_kernel(te_ref, nv_ref, xs_ref, w1_ref, w3_ref, w2_ref, ys_ref):
    del te_ref

    @pl.when(pl.program_id(0) < nv_ref[0])
    def _():
        x = xs_ref[...].astype(BF16)
        a = _dot(x, w1_ref[...])
        b = _dot(x, w3_ref[...])
        act = (a * jax.nn.sigmoid(a) * b).astype(BF16)
        ys_ref[...] = _dot(act, w2_ref[...])

    @pl.when(pl.program_id(0) >= nv_ref[0])
    def _():
        ys_ref[...] = jnp.zeros_like(ys_ref)


def moe_experts(xs, tile_expert, n_valid, w1, w3, w2, *, tmm):
    n_rows, d = xs.shape
    n_tiles = n_rows // tmm
    ff = w1.shape[-1]
    row_map = lambda i, te, nv: (jnp.minimum(i, nv[0] - 1), 0)
    w_map = lambda i, te, nv: (te[jnp.minimum(i, nv[0] - 1)], 0, 0)
    grid_spec = pltpu.PrefetchScalarGridSpec(
        num_scalar_prefetch=2,
        grid=(n_tiles,),
        in_specs=[
            pl.BlockSpec((tmm, d), row_map),
            pl.BlockSpec((None, d, ff), w_map),
            pl.BlockSpec((None, d, ff), w_map),
            pl.BlockSpec((None, ff, d), w_map),
        ],
        out_specs=pl.BlockSpec((tmm, d), lambda i, te, nv: (i, 0)),
    )
    return pl.pallas_call(
        _moe_expert_kernel,
        grid_spec=grid_spec,
        out_shape=jax.ShapeDtypeStruct((n_rows, d), F32),
        compiler_params=pltpu.CompilerParams(dimension_semantics=("arbitrary",)),
        name="moe_experts",
    )(tile_expert, n_valid, xs, w1, w3, w2)


def _moe_combine_kernel(pos_ref, x_ref, gate_ref, w_ref, ys_ref, o_ref, buf_ref, sem):
    tm = x_ref.shape[0]

    def start(r, c):
        _row_copy(ys_ref, pos_ref[0, r], buf_ref.at[0], r, sem).start()
        _row_copy(ys_ref, pos_ref[1, r], buf_ref.at[1], r, sem).start()
        return c

    lax.fori_loop(0, tm, start, 0)

    def wait(r, c):
        _row_copy(ys_ref, 0, buf_ref.at[0], 0, sem).wait()
        _row_copy(ys_ref, 0, buf_ref.at[1], 0, sem).wait()
        return c

    lax.fori_loop(0, tm, wait, 0)
    w = w_ref[...]
    y = w[:, 0:1] * buf_ref[0] + w[:, 1:2] * buf_ref[1]
    o_ref[...] = x_ref[...] + gate_ref[...] * y


def moe_combine(x, gate, wts, pos, ys, *, tm, n_ctx_tiles):
    nb, s_len, d = x.shape
    nt = s_len // tm
    return pl.pallas_call(
        _moe_combine_kernel,
        grid=(nb, nt),
        in_specs=[
            pl.BlockSpec((None, 2, tm), lambda b, j: (b * nt + j, 0, 0), memory_space=pltpu.SMEM),
            pl.BlockSpec((None, tm, d), lambda b, j: (b, j, 0)),
            pl.BlockSpec((None, 1, d), _mod_map(nb, nt, n_ctx_tiles)),
            pl.BlockSpec((tm, 2), lambda b, j: (b * nt + j, 0)),
            pl.BlockSpec(memory_space=pl.ANY),
        ],
        out_specs=pl.BlockSpec((None, tm, d), lambda b, j: (b, j, 0)),
        out_shape=jax.ShapeDtypeStruct((nb, s_len, d), F32),
        scratch_shapes=[pltpu.VMEM((2, tm, d), F32), pltpu.SemaphoreType.DMA(())],
        compiler_params=_SEQ2,
        name="moe_combine",
    )(pos, x, gate, wts, ys)


def sparse_moe(x, shift, scale, gate, norm_g, router_w, router_b, w1, w3, w2, *, n_ctx_tiles):
    nb, s_len, d = x.shape
    tm, tmm = TOKEN_TILE, MOE_ROW_TILE
    t = nb * s_len
    h, e, w, rank, cnt = moe_route(x, shift, scale, norm_g, router_w, router_b, tm=tm, n_ctx_tiles=n_ctx_tiles,
                                   n_groups=N_GROUPS, per_group=EXPERTS_PER_GROUP)
    counts = cnt[:, 0].astype(jnp.int32)
    padded = ((counts + tmm - 1) // tmm) * tmm
    ends = jnp.cumsum(padded)
    offs = ends - padded
    pos = offs[e] + rank
    n_tiles = (2 * t + N_EXPERTS * (tmm - 1)) // tmm + 1
    n_valid = (ends[-1] // tmm).astype(jnp.int32).reshape(1)
    tile_expert = jnp.minimum(jnp.searchsorted(ends, jnp.arange(n_tiles, dtype=jnp.int32) * tmm, side='right'),
                              N_EXPERTS - 1).astype(jnp.int32)
    xs = moe_dispatch(h.reshape(t, d), pos, n_tiles * tmm, tm=tm)
    ys = moe_experts(xs, tile_expert, n_valid, w1.astype(BF16), w3.astype(BF16), w2.astype(BF16), tmm=tmm)
    wts = w.transpose(0, 2, 1).reshape(t, 2)
    return moe_combine(x, gate, wts, pos, ys, tm=tm, n_ctx_tiles=n_ctx_tiles)


def rms_norm(x, g):
    xf = x.astype(jnp.float32)
    y = xf * lax.rsqrt(jnp.mean(xf * xf, axis=-1, keepdims=True) + RMS_EPS)
    return (y * g.astype(jnp.float32)).astype(x.dtype)


def modulate(h, shift, scale):
    return h * (1.0 + scale) + shift


def depthwise_conv(x, w, b, pad_left, pad_right):
    y = lax.conv_general_dilated(x, w[:, None, :].astype(x.dtype), window_strides=(1,),
                                 padding=[(pad_left, pad_right)],
                                 dimension_numbers=('NWC', 'WIO', 'NWC'),
                                 feature_group_count=x.shape[-1])
    return y + b.astype(x.dtype)


def hyena_filters(n, w1, b1, f1, w2, b2, f2, w3, decay):
    hy_emb = 2 * HY_BANDS + 1
    del hy_emb
    t = jnp.linspace(0.0, 1.0, n, dtype=jnp.float32)[:, None]
    w = (2.0 * math.pi / n) * jnp.arange(n, dtype=jnp.float32)[:, None]
    f = jnp.linspace(1e-4, HY_BANDS - 1, HY_BANDS, dtype=jnp.float32)[None, :]
    z = jnp.concatenate([t, jnp.cos(f * w), -jnp.sin(f * w)], axis=-1)
    a = jnp.sin(f1.astype(jnp.float32) * (z @ w1.astype(jnp.float32) + b1.astype(jnp.float32)))
    a = jnp.sin(f2.astype(jnp.float32) * (a @ w2.astype(jnp.float32) + b2.astype(jnp.float32)))
    h = (a @ w3.astype(jnp.float32)).reshape(n, 2 * HY_ORDER, HY_WIDTH)
    h = h * jnp.exp(-t[:, :, None] * jnp.abs(decay.astype(jnp.float32)))
    h = h.reshape(n, HY_ORDER, 2, HY_WIDTH)
    return h / (jnp.sum(jnp.abs(h), axis=(0, 2), keepdims=True) + 1e-6)


def two_sided_fftconv(u, h_fwd, h_bwd, skip):
    n = u.shape[1]
    u32 = u.astype(jnp.float32)
    spec_u = jnp.fft.rfft(u32, n=2 * n, axis=1)
    spec_h = jnp.fft.rfft(h_fwd, n=2 * n, axis=0) + jnp.conj(jnp.fft.rfft(h_bwd, n=2 * n, axis=0))
    y = jnp.fft.irfft(spec_u * spec_h[None], n=2 * n, axis=1)[:, :n]
    return (y + u32 * skip.astype(jnp.float32)).astype(u.dtype)


def hyena_operator(h, w_in, conv_w, conv_b, w1, b1, f1, w2, b2, f2, w3, decay, skip, w_out):
    n = h.shape[1]
    p = depthwise_conv(h @ w_in, conv_w, conv_b, 1, 1)
    v, x1, x2 = jnp.split(p, 3, axis=-1)
    filt = hyena_filters(n, w1, b1, f1, w2, b2, f2, w3, decay)
    z = v
    for order, gate in enumerate((x1, x2)):
        z = gate * two_sided_fftconv(z, filt[:, order, 0], filt[:, order, 1], skip[order])
    return z @ w_out


def _final_norm_kernel(x_ref, g_ref, o_ref):
    o_ref[...] = _rms(x_ref[...], g_ref[...])


def final_norm(x, g):
    b, n, d = x.shape
    tm = 1024
    out = pl.pallas_call(
        _final_norm_kernel,
        grid=(b * n // tm,),
        in_specs=[pl.BlockSpec((tm, d), lambda i: (i, 0)), pl.BlockSpec((1, d), lambda i: (0, 0))],
        out_specs=pl.BlockSpec((tm, d), lambda i: (i, 0)),
        out_shape=jax.ShapeDtypeStruct((b * n, d), x.dtype),
        name="final_norm",
    )(x.reshape(b * n, d), g.reshape(1, d))
    return out.reshape(b, n, d)


def kernel(x, c, ctx, c_ctx, ada_w, ada_b, norm1_g, norm2_g, final_g, ev_w_in, mla_q_norm_g, mla_kv_norm_g, mla_w_uq, mla_w_ukv, lru_conv_w, lru_conv_b, lru_w_a, lru_b_a, lru_w_x, lru_b_x, lru_lambda, ev_w_out, od_w_in, hy_conv_w, hy_conv_b, hy_w1, hy_b1, hy_freq1, hy_w2, hy_b2, hy_freq2, hy_w3, hy_decay, hy_skip, od_w_out, router_w, router_b, moe_w1, moe_w3, moe_w2):
    nb, n_lat, d = x.shape
    n_ctx = ctx.shape[1]
    n_ctx_tiles = n_ctx // TOKEN_TILE
    cond = jnp.concatenate([jax.nn.silu(c), jax.nn.silu(c_ctx)[None, :]], axis=0)
    cond = jnp.pad(cond, ((0, 2 * SUBLANES - nb - 1), (0, 0)))

    def mod_rows(layer):
        mod = rows_matmul(cond, ada_w[layer], ada_b[layer], tn=D_MODEL)[:nb + 1]
        return [m[:, None, :] for m in jnp.split(mod, 6, axis=-1)]

    sh1, sc1, g1, sh2, sc2, g2 = mod_rows(0)
    x_all = jnp.concatenate([x, ctx], axis=1)
    p0 = dict(w_in=ev_w_in[0], q_norm_g=mla_q_norm_g[0], kv_norm_g=mla_kv_norm_g[0], w_uq=mla_w_uq[0],
              w_ukv=mla_w_ukv[0], conv_w=lru_conv_w[0], conv_b=lru_conv_b[0], w_a=lru_w_a[0], b_a=lru_b_a[0],
              w_x=lru_w_x[0], b_x=lru_b_x[0], lam=lru_lambda[0], w_out=ev_w_out[0])
    x_all = even_mixer(x_all, sh1, sc1, g1, norm1_g[0], p0, n_lat=n_lat, n_ctx=n_ctx)
    x_all = sparse_moe(x_all, sh2, sc2, g2, norm2_g[0], router_w, router_b, moe_w1[0], moe_w3[0], moe_w2[0],
                       n_ctx_tiles=n_ctx_tiles)
    x = x_all[:, :n_lat]

    sh1, sc1, g1, sh2, sc2, g2 = [m[:nb] for m in mod_rows(1)]
    h_lat = modulate(rms_norm(x, norm1_g[1]), sh1, sc1)
    o_lat = hyena_operator(h_lat, od_w_in[0], hy_conv_w[0], hy_conv_b[0], hy_w1[0], hy_b1[0], hy_freq1[0],
                           hy_w2[0], hy_b2[0], hy_freq2[0], hy_w3[0], hy_decay[0], hy_skip[0], od_w_out[0])
    x = x + g1 * o_lat
    x = sparse_moe(x, sh2, sc2, g2, norm2_g[1], router_w, router_b, moe_w1[1], moe_w3[1], moe_w2[1], n_ctx_tiles=0)
    return final_norm(x, final_g)
```

```python
import functools
import math

import jax
import jax.numpy as jnp
import numpy as np
from jax import lax
from jax.experimental import pallas as pl
from jax.experimental.pallas import tpu as pltpu

D_MODEL = 1024
DEPTH = 2
GRID_W = 64
RMS_EPS = 1e-6

MLA_HEADS = 8
MLA_Q_LORA = 384
MLA_KV_LORA = 256
MLA_NOPE = 64
MLA_ROPE = 32
MLA_V = 64
MLA_QK = MLA_NOPE + MLA_ROPE
ROPE_PAIRS = MLA_ROPE // 4
ROPE_BASE = 10000.0

LRU_WIDTH = 512
LRU_BLOCKS = 8
LRU_BLOCK_DIM = LRU_WIDTH // LRU_BLOCKS
LRU_C = 8.0
LRU_CONV = 4

HY_WIDTH = D_MODEL
HY_ORDER = 2
HY_BANDS = 16
HY_HIDDEN = 64

N_EXPERTS = 16
N_GROUPS = 4
EXPERTS_PER_GROUP = N_EXPERTS // N_GROUPS

LANES = 128
SUBLANES = 8
HEAD_PAD = LANES

TOKEN_TILE = 256
MOE_ROW_TILE = 512
ATTN_Q_TILE = 512
LRU_CHUNK = 256

BF16 = jnp.bfloat16
F32 = jnp.float32
_HI = lax.Precision.HIGHEST
_NEG_INF = float('-inf')


def _dot(a, b):
    return jnp.dot(a, b, preferred_element_type=F32)


def _rms(x, g):
    return x * lax.rsqrt(jnp.mean(x * x, axis=-1, keepdims=True) + RMS_EPS) * g


def _mod_map(nb, nt, n_ctx_tiles):
    if n_ctx_tiles:
        return lambda b, j: (jnp.where(j >= nt - n_ctx_tiles, nb, b), 0, 0)
    return lambda b, j: (b, 0, 0)


_SEQ2 = pltpu.CompilerParams(dimension_semantics=("arbitrary", "arbitrary"))


def _rows_matmul_kernel(x_ref, w_ref, b_ref, o_ref):
    o_ref[...] = _dot(x_ref[...].astype(BF16), w_ref[...].astype(BF16)) + b_ref[...]


def rows_matmul(x, w, b, *, tn):
    m, k = x.shape
    n = w.shape[1]
    return pl.pallas_call(
        _rows_matmul_kernel,
        grid=(n // tn,),
        in_specs=[pl.BlockSpec((m, k), lambda j: (0, 0)),
                  pl.BlockSpec((k, tn), lambda j: (0, j)),
                  pl.BlockSpec((1, tn), lambda j: (0, j))],
        out_specs=pl.BlockSpec((m, tn), lambda j: (0, j)),
        out_shape=jax.ShapeDtypeStruct((m, n), F32),
        name="rows_matmul",
    )(x, w, b.reshape(1, n))


def _rot_cols(w):
    p = ROPE_PAIRS
    return jnp.concatenate([-w[:, p:2 * p], w[:, 0:p], -w[:, 3 * p:4 * p], w[:, 2 * p:3 * p]], axis=1)


def _head_pad_cols(w, width):
    k = w.shape[0]
    w = w.reshape(k, MLA_HEADS, width)
    return jnp.pad(w, ((0, 0), (0, 0), (0, HEAD_PAD - width))).reshape(k, MLA_HEADS * HEAD_PAD)


def prep_even_weights(w_in, w_uq, w_ukv, w_out):
    d = w_in.shape[0]
    s1 = MLA_Q_LORA + MLA_KV_LORA
    s2 = s1 + MLA_ROPE
    w_kr = w_in[:, s1:s2]
    place = lambda w: jnp.pad(w, ((0, 0), (MLA_NOPE, HEAD_PAD - MLA_QK)))
    w_big = jnp.concatenate([w_in[:, :s1], w_in[:, s2:], place(w_kr), place(_rot_cols(w_kr))], axis=1).astype(BF16)
    uq = w_uq.reshape(MLA_Q_LORA, MLA_HEADS, MLA_QK)
    wq_a = _head_pad_cols(w_uq, MLA_QK)
    uq_rot = jnp.stack([_rot_cols(uq[:, h, MLA_NOPE:]) for h in range(MLA_HEADS)], axis=1)
    wq_b = jnp.pad(uq_rot, ((0, 0), (0, 0), (MLA_NOPE, HEAD_PAD - MLA_QK))).reshape(MLA_Q_LORA, MLA_HEADS * HEAD_PAD)
    ukv = w_ukv.reshape(MLA_KV_LORA, MLA_HEADS, MLA_NOPE + MLA_V)
    wk = _head_pad_cols(ukv[:, :, :MLA_NOPE].reshape(MLA_KV_LORA, -1), MLA_NOPE)
    wv = _head_pad_cols(ukv[:, :, MLA_NOPE:].reshape(MLA_KV_LORA, -1), MLA_V)
    w_kv = jnp.concatenate([wk, wv], axis=1)
    att_rows = MLA_HEADS * MLA_V
    wo_att = jnp.pad(w_out[:att_rows].reshape(MLA_HEADS, MLA_V, d),
                     ((0, 0), (0, HEAD_PAD - MLA_V), (0, 0))).reshape(MLA_HEADS * HEAD_PAD, d)
    return dict(w_big=w_big, wq_a=wq_a.astype(BF16), wq_b=wq_b.astype(BF16), w_kv=w_kv.astype(BF16),
                wo_att=wo_att.astype(BF16), wo_lru=w_out[att_rows:].astype(BF16))


def rope_tables(n_lat, n_ctx):
    rows = n_lat // GRID_W
    row = jnp.repeat(jnp.arange(rows), GRID_W)
    col = jnp.tile(jnp.arange(GRID_W), rows)
    inv_freq = ROPE_BASE ** (-jnp.arange(ROPE_PAIRS, dtype=F32) / ROPE_PAIRS)
    ang = jnp.stack([row, col], axis=-1).astype(F32)[:, :, None] * inv_freq
    cos, sin = jnp.cos(ang), jnp.sin(ang)
    c32 = jnp.concatenate([cos[:, 0], cos[:, 0], cos[:, 1], cos[:, 1]], axis=-1)
    s32 = jnp.concatenate([sin[:, 0], sin[:, 0], sin[:, 1], sin[:, 1]], axis=-1)
    c32 = jnp.concatenate([c32, jnp.ones((n_ctx, MLA_ROPE), F32)], axis=0)
    s32 = jnp.concatenate([s32, jnp.zeros((n_ctx, MLA_ROPE), F32)], axis=0)
    n = n_lat + n_ctx
    pad_hi = jnp.zeros((n, HEAD_PAD - MLA_QK), F32)
    scale = MLA_QK ** -0.5
    cq = jnp.concatenate([jnp.full((n, MLA_NOPE), scale, F32), c32 * scale, pad_hi], axis=-1)
    sq = jnp.concatenate([jnp.zeros((n, MLA_NOPE), F32), s32 * scale, pad_hi], axis=-1)
    ck = jnp.concatenate([jnp.zeros((n, MLA_NOPE), F32), c32, pad_hi], axis=-1)
    sk = jnp.concatenate([jnp.zeros((n, MLA_NOPE), F32), s32, pad_hi], axis=-1)
    return cq, sq, ck, sk


def _even_in_kernel(x_ref, sh_ref, sc_ref, g_ref, wbig_ref, qg_ref, kvg_ref, wqa_ref, wqb_ref, wkv_ref,
                    cq_ref, sq_ref, ck_ref, sk_ref, q_ref, k_ref, v_ref, ux_ref, ug_ref):
    h = (_rms(x_ref[...], g_ref[...]) * (1.0 + sc_ref[...]) + sh_ref[...]).astype(BF16)
    p = _dot(h, wbig_ref[...])
    o1 = MLA_Q_LORA
    o2 = o1 + MLA_KV_LORA
    o3 = o2 + LRU_WIDTH
    o4 = o3 + LRU_WIDTH
    o5 = o4 + HEAD_PAD
    ux_ref[...] = p[:, o2:o3]
    ug_ref[...] = p[:, o3:o4]
    cqn = _rms(p[:, :o1], qg_ref[...]).astype(BF16)
    rep = lambda t: jnp.concatenate([t] * MLA_HEADS, axis=-1)
    q = _dot(cqn, wqa_ref[...]) * rep(cq_ref[...]) + _dot(cqn, wqb_ref[...]) * rep(sq_ref[...])
    q_ref[...] = q.astype(BF16)
    ckvn = _rms(p[:, o1:o2], kvg_ref[...]).astype(BF16)
    kv = _dot(ckvn, wkv_ref[...])
    k_rope = p[:, o4:o5] * ck_ref[...] + p[:, o5:] * sk_ref[...]
    hw = MLA_HEADS * HEAD_PAD
    k_ref[...] = (kv[:, :hw] + rep(k_rope)).astype(BF16)
    v_ref[...] = kv[:, hw:].astype(BF16)


def even_in_proj(x, shift, scale, norm_g, wts, q_norm_g, kv_norm_g, tables, *, tm, n_ctx_tiles):
    nb, s_len, d = x.shape
    nt = s_len // tm
    hw = MLA_HEADS * HEAD_PAD
    mod_map = _mod_map(nb, nt, n_ctx_tiles)
    const = lambda b, j: (0, 0)
    tok = lambda b, j: (b, j, 0)
    tab = lambda b, j: (j, 0)
    full = lambda a: pl.BlockSpec(a.shape, const)
    row = lambda v: v.reshape(1, -1)
    args = [x, shift, scale, row(norm_g), wts['w_big'], row(q_norm_g), row(kv_norm_g),
            wts['wq_a'], wts['wq_b'], wts['w_kv']]
    in_specs = [pl.BlockSpec((None, tm, d), tok), pl.BlockSpec((None, 1, d), mod_map),
                pl.BlockSpec((None, 1, d), mod_map)]
    in_specs += [full(a) for a in args[3:]]
    in_specs += [pl.BlockSpec((tm, HEAD_PAD), tab)] * 4
    out_dims = [(hw, BF16), (hw, BF16), (hw, BF16), (LRU_WIDTH, F32), (LRU_WIDTH, F32)]
    return pl.pallas_call(
        _even_in_kernel,
        grid=(nb, nt),
        in_specs=in_specs,
        out_specs=[pl.BlockSpec((None, tm, w), tok) for w, _ in out_dims],
        out_shape=[jax.ShapeDtypeStruct((nb, s_len, w), dt) for w, dt in out_dims],
        compiler_params=_SEQ2,
        name="even_in_proj",
    )(*args, *tables)


def _attn_kernel(q_ref, k_ref, v_ref, o_ref):
    s = lax.dot_general(q_ref[...], k_ref[...], (((1,), (1,)), ((), ())), preferred_element_type=F32)
    m = jnp.max(s, axis=-1, keepdims=True)
    p = jnp.exp(s - m)
    l = jnp.sum(p, axis=-1, keepdims=True)
    o = _dot(p.astype(BF16), v_ref[...])
    o_ref[...] = (o / l).astype(o_ref.dtype)


def attention(q, k, v, *, q_start, n_q, k_start, n_k, tq):
    nb = q.shape[0]
    qo = q_start // tq
    ko = k_start // n_k
    return pl.pallas_call(
        _attn_kernel,
        grid=(nb, MLA_HEADS, n_q // tq),
        in_specs=[pl.BlockSpec((None, tq, HEAD_PAD), lambda b, h, i: (b, qo + i, h)),
                  pl.BlockSpec((None, n_k, HEAD_PAD), lambda b, h, i: (b, ko, h)),
                  pl.BlockSpec((None, n_k, HEAD_PAD), lambda b, h, i: (b, ko, h))],
        out_specs=pl.BlockSpec((None, tq, HEAD_PAD), lambda b, h, i: (b, i, h)),
        out_shape=jax.ShapeDtypeStruct((nb, n_q, MLA_HEADS * HEAD_PAD), BF16),
        compiler_params=pltpu.CompilerParams(dimension_semantics=("arbitrary", "arbitrary", "arbitrary")),
        name="attention",
    )(q, k, v)


def _scan_block(a, b, h_prev, row, reverse):
    for s in (1, 2, 4):
        sh = SUBLANES - s if reverse else s
        a_s = pltpu.roll(a, sh, 0)
        b_s = pltpu.roll(b, sh, 0)
        ok = (row < SUBLANES - s) if reverse else (row >= s)
        b = jnp.where(ok, a * b_s + b, b)
        a = jnp.where(ok, a * a_s, a)
    return a * h_prev + b


def _rglru_kernel(ux_ref, ug_ref, cw_ref, cb_ref, wa_ref, ba_ref, wx_ref, bx_ref, lam_ref, o_ref,
                  xp_ref, a_ref, b_ref, hf_ref, hb_ref, *, n_lat, n_ctx):
    lanes = ux_ref.shape[-1]
    pad = SUBLANES
    zeros = jnp.zeros((pad, lanes), F32)
    lat0 = pad
    ctx0 = 2 * pad + n_lat
    xp_ref[0:pad, :] = zeros
    xp_ref[lat0 + n_lat:ctx0, :] = zeros
    xp_ref[ctx0 + n_ctx:ctx0 + n_ctx + pad, :] = zeros
    xp_ref[lat0:lat0 + n_lat, :] = ux_ref[0:n_lat, :]
    xp_ref[ctx0:ctx0 + n_ctx, :] = ux_ref[n_lat:n_lat + n_ctx, :]
    cw = cw_ref[...]
    n_chunks = (n_lat + n_ctx) // LRU_CHUNK
    n_lat_chunks = n_lat // LRU_CHUNK
    n_win = LRU_CHUNK + 2 * pad

    def coeffs(c, carry):
        src = pl.multiple_of(c * LRU_CHUNK + jnp.where(c >= n_lat_chunks, pad, 0), SUBLANES)
        dst = pl.multiple_of(c * LRU_CHUNK, SUBLANES)
        win = xp_ref[pl.ds(src, n_win), :]
        mid = lambda t: t[pad:pad + LRU_CHUNK]
        u = cb_ref[...] + cw[2:3, :] * mid(win)
        u = u + cw[0:1, :] * mid(pltpu.roll(win, 2, 0))
        u = u + cw[1:2, :] * mid(pltpu.roll(win, 1, 0))
        u = u + cw[3:4, :] * mid(pltpu.roll(win, n_win - 1, 0))
        ub = u.astype(BF16)
        for d in range(2):
            r = jax.nn.sigmoid(_dot(ub, wa_ref[d]) + ba_ref[d])
            i = jax.nn.sigmoid(_dot(ub, wx_ref[d]) + bx_ref[d])
            a = jnp.exp(lam_ref[d] * r)
            a_ref[d, pl.ds(dst, LRU_CHUNK), :] = a
            b_ref[d, pl.ds(dst, LRU_CHUNK), :] = jnp.sqrt(1.0 - a * a) * (i * u)
        return carry

    lax.fori_loop(0, n_chunks, coeffs, 0)
    row = lax.broadcasted_iota(jnp.int32, (SUBLANES, lanes), 0)

    def make_step(first_blk, n_blk):
        def step(t, carry):
            h_f, h_b = carry
            rf = pl.multiple_of((first_blk + t) * SUBLANES, SUBLANES)
            rb = pl.multiple_of((first_blk + n_blk - 1 - t) * SUBLANES, SUBLANES)
            out_f = _scan_block(a_ref[0, pl.ds(rf, SUBLANES), :], b_ref[0, pl.ds(rf, SUBLANES), :], h_f, row, False)
            out_b = _scan_block(a_ref[1, pl.ds(rb, SUBLANES), :], b_ref[1, pl.ds(rb, SUBLANES), :], h_b, row, True)
            hf_ref[pl.ds(rf, SUBLANES), :] = out_f
            hb_ref[pl.ds(rb, SUBLANES), :] = out_b
            h_f = jnp.broadcast_to(out_f[SUBLANES - 1:SUBLANES, :], out_f.shape)
            h_b = jnp.broadcast_to(out_b[0:1, :], out_b.shape)
            return h_f, h_b
        return step

    state = (jnp.zeros((SUBLANES, lanes), F32), jnp.zeros((SUBLANES, lanes), F32))
    state = lax.fori_loop(0, n_ctx // SUBLANES, make_step(n_lat // SUBLANES, n_ctx // SUBLANES), state)
    lax.fori_loop(0, n_lat // SUBLANES, make_step(0, n_lat // SUBLANES), state)

    def gate(c, carry):
        r0 = pl.multiple_of(c * LRU_CHUNK, SUBLANES)
        g = ug_ref[pl.ds(r0, LRU_CHUNK), :]
        gelu = 0.5 * g * (1.0 + jnp.tanh(math.sqrt(2.0 / math.pi) * (g + 0.044715 * (g * g * g))))
        y = hf_ref[pl.ds(r0, LRU_CHUNK), :] + hb_ref[pl.ds(r0, LRU_CHUNK), :]
        o_ref[pl.ds(r0, LRU_CHUNK), :] = (y * gelu).astype(o_ref.dtype)
        return carry

    lax.fori_loop(0, n_chunks, gate, 0)


def _block_diag(w):
    per = LANES // LRU_BLOCK_DIM
    nt = LRU_BLOCKS // per
    w = w.reshape(2, nt, per, LRU_BLOCK_DIM, LRU_BLOCK_DIM)
    eye = jnp.eye(per, dtype=w.dtype)
    out = jnp.einsum('dtpij,pq->dtpiqj', w, eye)
    return out.reshape(2, nt, LANES, LANES)


def rglru(ux, ug, conv_w, conv_b, w_a, b_a, w_x, b_x, lam, *, n_lat, n_ctx):
    nb, s_len, width = ux.shape
    nt = width // LANES
    wa = _block_diag(w_a).astype(BF16).transpose(1, 0, 2, 3)
    wx = _block_diag(w_x).astype(BF16).transpose(1, 0, 2, 3)
    lam_c = -LRU_C * jax.nn.softplus(-lam.astype(F32))
    lane3 = lambda v: v.reshape(2, 1, width)
    seq = pl.BlockSpec((None, s_len, LANES), lambda b, c: (b, 0, c))
    vec = lambda rows: pl.BlockSpec((rows, LANES), lambda b, c: (0, c))
    dvec = pl.BlockSpec((2, 1, LANES), lambda b, c: (0, 0, c))
    wspec = pl.BlockSpec((None, 2, LANES, LANES), lambda b, c: (c, 0, 0, 0))
    return pl.pallas_call(
        functools.partial(_rglru_kernel, n_lat=n_lat, n_ctx=n_ctx),
        grid=(nb, nt),
        in_specs=[seq, seq, vec(LRU_CONV), vec(1), wspec, dvec, wspec, dvec, dvec],
        out_specs=seq,
        out_shape=jax.ShapeDtypeStruct((nb, s_len, width), BF16),
        scratch_shapes=[pltpu.VMEM((s_len + 3 * SUBLANES, LANES), F32),
                        pltpu.VMEM((2, s_len, LANES), F32), pltpu.VMEM((2, s_len, LANES), F32),
                        pltpu.VMEM((s_len, LANES), F32), pltpu.VMEM((s_len, LANES), F32)],
        compiler_params=_SEQ2,
        name="rglru",
    )(ux, ug, conv_w, conv_b.reshape(1, width), wa, lane3(b_a), wx, lane3(b_x), lane3(lam_c))


def _even_out_kernel(x_ref, gate_ref, att_ref, lru_ref, wa_ref, wl_ref, o_ref):
    y = _dot(att_ref[...], wa_ref[...]) + _dot(lru_ref[...], wl_ref[...])
    o_ref[...] = x_ref[...] + gate_ref[...] * y


def even_out_proj(x, gate, att, lru, wo_att, wo_lru, *, tm, n_ctx_tiles):
    nb, s_len, d = x.shape
    nt = s_len // tm
    tok = lambda b, j: (b, j, 0)
    const = lambda b, j: (0, 0)
    return pl.pallas_call(
        _even_out_kernel,
        grid=(nb, nt),
        in_specs=[pl.BlockSpec((None, tm, d), tok), pl.BlockSpec((None, 1, d), _mod_map(nb, nt, n_ctx_tiles)),
                  pl.BlockSpec((None, tm, att.shape[-1]), tok), pl.BlockSpec((None, tm, lru.shape[-1]), tok),
                  pl.BlockSpec(wo_att.shape, const), pl.BlockSpec(wo_lru.shape, const)],
        out_specs=pl.BlockSpec((None, tm, d), tok),
        out_shape=jax.ShapeDtypeStruct((nb, s_len, d), F32),
        compiler_params=_SEQ2,
        name="even_out_proj",
    )(x, gate, att, lru, wo_att, wo_lru)


def even_mixer(x_all, shift, scale, gate, norm_g, p, *, n_lat, n_ctx):
    n_ctx_tiles = n_ctx // TOKEN_TILE
    tables = rope_tables(n_lat, n_ctx)
    wts = prep_even_weights(p['w_in'], p['w_uq'], p['w_ukv'], p['w_out'])
    q, k, v, ux, ug = even_in_proj(x_all, shift, scale, norm_g, wts, p['q_norm_g'], p['kv_norm_g'], tables,
                                   tm=TOKEN_TILE, n_ctx_tiles=n_ctx_tiles)
    att_l = attention(q, k, v, q_start=0, n_q=n_lat, k_start=0, n_k=n_lat + n_ctx, tq=ATTN_Q_TILE)
    att_c = attention(q, k, v, q_start=n_lat, n_q=n_ctx, k_start=n_lat, n_k=n_ctx, tq=n_ctx)
    att = jnp.concatenate([att_l, att_c], axis=1)
    lru = rglru(ux, ug, p['conv_w'], p['conv_b'], p['w_a'], p['b_a'], p['w_x'], p['b_x'], p['lam'],
                n_lat=n_lat, n_ctx=n_ctx)
    return even_out_proj(x_all, gate, att, lru, wts['wo_att'], wts['wo_lru'], tm=TOKEN_TILE, n_ctx_tiles=n_ctx_tiles)


def _first_argmax(vals):
    best = vals[0]
    idx = jnp.zeros(vals[0].shape, jnp.int32)
    for j in range(1, len(vals)):
        better = vals[j] > best
        idx = jnp.where(better, j, idx)
        best = jnp.where(better, vals[j], best)
    return idx, best


def _select(idx, vals):
    out = vals[0]
    for j in range(1, len(vals)):
        out = jnp.where(idx == j, vals[j], out)
    return out


def _moe_route_kernel(x_ref, sh_ref, sc_ref, g_ref, rwt_ref, rb_ref,
                      h_ref, e_ref, w_ref, rank_ref, cnt_ref, carry_ref, *, n_groups, per_group):
    first = jnp.logical_and(pl.program_id(0) == 0, pl.program_id(1) == 0)

    @pl.when(first)
    def _():
        carry_ref[...] = jnp.zeros_like(carry_ref)

    tm = x_ref.shape[0]
    n_exp = n_groups * per_group
    h = _rms(x_ref[...], g_ref[...]) * (1.0 + sc_ref[...]) + sh_ref[...]
    h_ref[...] = h
    logits = lax.dot_general(rwt_ref[...], h, (((1,), (1,)), ((), ())),
                             precision=_HI, preferred_element_type=F32)
    s = jax.nn.sigmoid(logits)
    sel = s + rb_ref[...]
    sel_rows = [sel[e:e + 1, :] for e in range(n_exp)]
    s_rows = [s[e:e + 1, :] for e in range(n_exp)]
    g_scores = []
    for g in range(n_groups):
        r = sel_rows[g * per_group:(g + 1) * per_group]
        pair_sums = [r[a] + r[b] for a in range(per_group) for b in range(a + 1, per_group)]
        g_scores.append(functools.reduce(jnp.maximum, pair_sums))
    g_idx, _ = _first_argmax(g_scores)
    v = [_select(g_idx, [sel_rows[g * per_group + j] for g in range(n_groups)]) for j in range(per_group)]
    sv = [_select(g_idx, [s_rows[g * per_group + j] for g in range(n_groups)]) for j in range(per_group)]
    i1, _ = _first_argmax(v)
    i2, _ = _first_argmax([jnp.full_like(v[0], _NEG_INF)]
                          + [jnp.where(i1 == j, _NEG_INF, v[j]) for j in range(per_group)])
    i2 = i2 - 1
    w1 = _select(i1, sv)
    w2 = _select(i2, sv)
    wsum = w1 + w2
    e1 = g_idx * per_group + i1
    e2 = g_idx * per_group + i2
    e_ref[0:1, :] = e1
    e_ref[1:2, :] = e2
    w_ref[0:1, :] = w1 / wsum
    w_ref[1:2, :] = w2 / wsum
    e_iota = lax.broadcasted_iota(jnp.int32, (n_exp, tm), 0)
    oh1 = e_iota == e1
    oh2 = e_iota == e2
    m = jnp.where(oh1, 1.0, jnp.where(oh2, 1.0, 0.0))
    r_i = lax.broadcasted_iota(jnp.int32, (tm, tm), 0)
    c_i = lax.broadcasted_iota(jnp.int32, (tm, tm), 1)
    upper = jnp.where(r_i < c_i, 1.0, 0.0).astype(BF16)
    pre = _dot(m.astype(BF16), upper)
    tot = pre + carry_ref[:, 0:1]
    rank_ref[0:1, :] = jnp.sum(jnp.where(oh1, tot, 0.0), axis=0, keepdims=True).astype(jnp.int32)
    rank_ref[1:2, :] = jnp.sum(jnp.where(oh2, tot, 0.0), axis=0, keepdims=True).astype(jnp.int32)
    new_carry = carry_ref[...] + jnp.sum(m, axis=1, keepdims=True)
    carry_ref[...] = new_carry
    cnt_ref[...] = new_carry


def moe_route(x, shift, scale, norm_g, router_w, router_b, *, tm, n_ctx_tiles, n_groups, per_group):
    nb, s_len, d = x.shape
    nt = s_len // tm
    n_exp = n_groups * per_group
    rwt = router_w.T
    rb = router_b.reshape(n_exp, 1)
    mod_map = _mod_map(nb, nt, n_ctx_tiles)
    tok_map = lambda b, j: (b * nt + j, 0, 0)
    return pl.pallas_call(
        functools.partial(_moe_route_kernel, n_groups=n_groups, per_group=per_group),
        grid=(nb, nt),
        in_specs=[
            pl.BlockSpec((None, tm, d), lambda b, j: (b, j, 0)),
            pl.BlockSpec((None, 1, d), mod_map),
            pl.BlockSpec((None, 1, d), mod_map),
            pl.BlockSpec((1, d), lambda b, j: (0, 0)),
            pl.BlockSpec((n_exp, d), lambda b, j: (0, 0)),
            pl.BlockSpec((n_exp, 1), lambda b, j: (0, 0)),
        ],
        out_specs=[
            pl.BlockSpec((None, tm, d), lambda b, j: (b, j, 0)),
            pl.BlockSpec((None, 2, tm), tok_map),
            pl.BlockSpec((None, 2, tm), tok_map),
            pl.BlockSpec((None, 2, tm), tok_map),
            pl.BlockSpec((n_exp, LANES), lambda b, j: (0, 0)),
        ],
        out_shape=[
            jax.ShapeDtypeStruct((nb, s_len, d), F32),
            jax.ShapeDtypeStruct((nb * nt, 2, tm), jnp.int32),
            jax.ShapeDtypeStruct((nb * nt, 2, tm), F32),
            jax.ShapeDtypeStruct((nb * nt, 2, tm), jnp.int32),
            jax.ShapeDtypeStruct((n_exp, LANES), F32),
        ],
        scratch_shapes=[pltpu.VMEM((n_exp, LANES), F32)],
        compiler_params=_SEQ2,
        name="moe_route",
    )(x, shift, scale, norm_g.reshape(1, d), rwt, rb)


def _row_copy(src_ref, src_row, dst_ref, dst_row, sem):
    return pltpu.make_async_copy(src_ref.at[pl.ds(src_row, 1)], dst_ref.at[pl.ds(dst_row, 1)], sem)


def _moe_dispatch_kernel(pos_ref, h_ref, xs_in_ref, xs_ref, sem):
    del xs_in_ref
    tm = h_ref.shape[0]

    def start(r, c):
        _row_copy(h_ref, r, xs_ref, pos_ref[0, r], sem).start()
        _row_copy(h_ref, r, xs_ref, pos_ref[1, r], sem).start()
        return c

    lax.fori_loop(0, tm, start, 0)

    def wait(r, c):
        _row_copy(h_ref, 0, xs_ref, 0, sem).wait()
        _row_copy(h_ref, 0, xs_ref, 0, sem).wait()
        return c

    lax.fori_loop(0, tm, wait, 0)


def moe_dispatch(h2d, pos, n_rows, *, tm):
    t, d = h2d.shape
    zeros = jnp.zeros((n_rows, d), h2d.dtype)
    return pl.pallas_call(
        _moe_dispatch_kernel,
        grid=(t // tm,),
        in_specs=[
            pl.BlockSpec((None, 2, tm), lambda i: (i, 0, 0), memory_space=pltpu.SMEM),
            pl.BlockSpec((tm, d), lambda i: (i, 0)),
            pl.BlockSpec(memory_space=pl.ANY),
        ],
        out_specs=pl.BlockSpec(memory_space=pl.ANY),
        out_shape=jax.ShapeDtypeStruct((n_rows, d), h2d.dtype),
        scratch_shapes=[pltpu.SemaphoreType.DMA(())],
        input_output_aliases={2: 0},
        compiler_params=pltpu.CompilerParams(dimension_semantics=("arbitrary",), has_side_effects=True),
        name="moe_dispatch",
    )(pos, h2d, zeros)


def _moe_expert_kernel(te_ref, nv_ref, xs_ref, w1_ref, w3_ref, w2_ref, ys_ref):
    del te_ref

    @pl.when(pl.program_id(0) < nv_ref[0])
    def _():
        x = xs_ref[...].astype(BF16)
        a = _dot(x, w1_ref[...])
        b = _dot(x, w3_ref[...])
        act = (a * jax.nn.sigmoid(a) * b).astype(BF16)
        ys_ref[...] = _dot(act, w2_ref[...])

    @pl.when(pl.program_id(0) >= nv_ref[0])
    def _():
        ys_ref[...] = jnp.zeros_like(ys_ref)


def moe_experts(xs, tile_expert, n_valid, w1, w3, w2, *, tmm):
    n_rows, d = xs.shape
    n_tiles = n_rows // tmm
    ff = w1.shape[-1]
    row_map = lambda i, te, nv: (jnp.minimum(i, nv[0] - 1), 0)
    w_map = lambda i, te, nv: (te[jnp.minimum(i, nv[0] - 1)], 0, 0)
    grid_spec = pltpu.PrefetchScalarGridSpec(
        num_scalar_prefetch=2,
        grid=(n_tiles,),
        in_specs=[
            pl.BlockSpec((tmm, d), row_map),
            pl.BlockSpec((None, d, ff), w_map),
            pl.BlockSpec((None, d, ff), w_map),
            pl.BlockSpec((None, ff, d), w_map),
        ],
        out_specs=pl.BlockSpec((tmm, d), lambda i, te, nv: (i, 0)),
    )
    return pl.pallas_call(
        _moe_expert_kernel,
        grid_spec=grid_spec,
        out_shape=jax.ShapeDtypeStruct((n_rows, d), F32),
        compiler_params=pltpu.CompilerParams(dimension_semantics=("arbitrary",)),
        name="moe_experts",
    )(tile_expert, n_valid, xs, w1, w3, w2)


def _moe_combine_kernel(pos_ref, x_ref, gate_ref, w_ref, ys_ref, o_ref, buf_ref, sem):
    tm = x_ref.shape[0]

    def start(r, c):
        _row_copy(ys_ref, pos_ref[0, r], buf_ref.at[0], r, sem).start()
        _row_copy(ys_ref, pos_ref[1, r], buf_ref.at[1], r, sem).start()
        return c

    lax.fori_loop(0, tm, start, 0)

    def wait(r, c):
        _row_copy(ys_ref, 0, buf_ref.at[0], 0, sem).wait()
        _row_copy(ys_ref, 0, buf_ref.at[1], 0, sem).wait()
        return c

    lax.fori_loop(0, tm, wait, 0)
    w = w_ref[...]
    y = w[:, 0:1] * buf_ref[0] + w[:, 1:2] * buf_ref[1]
    o_ref[...] = x_ref[...] + gate_ref[...] * y


def moe_combine(x, gate, wts, pos, ys, *, tm, n_ctx_tiles):
    nb, s_len, d = x.shape
    nt = s_len // tm
    return pl.pallas_call(
        _moe_combine_kernel,
        grid=(nb, nt),
        in_specs=[
            pl.BlockSpec((None, 2, tm), lambda b, j: (b * nt + j, 0, 0), memory_space=pltpu.SMEM),
            pl.BlockSpec((None, tm, d), lambda b, j: (b, j, 0)),
            pl.BlockSpec((None, 1, d), _mod_map(nb, nt, n_ctx_tiles)),
            pl.BlockSpec((tm, 2), lambda b, j: (b * nt + j, 0)),
            pl.BlockSpec(memory_space=pl.ANY),
        ],
        out_specs=pl.BlockSpec((None, tm, d), lambda b, j: (b, j, 0)),
        out_shape=jax.ShapeDtypeStruct((nb, s_len, d), F32),
        scratch_shapes=[pltpu.VMEM((2, tm, d), F32), pltpu.SemaphoreType.DMA(())],
        compiler_params=_SEQ2,
        name="moe_combine",
    )(pos, x, gate, wts, ys)


def sparse_moe(x, shift, scale, gate, norm_g, router_w, router_b, w1, w3, w2, *, n_ctx_tiles):
    nb, s_len, d = x.shape
    tm, tmm = TOKEN_TILE, MOE_ROW_TILE
    t = nb * s_len
    h, e, w, rank, cnt = moe_route(x, shift, scale, norm_g, router_w, router_b, tm=tm, n_ctx_tiles=n_ctx_tiles,
                                   n_groups=N_GROUPS, per_group=EXPERTS_PER_GROUP)
    counts = cnt[:, 0].astype(jnp.int32)
    padded = ((counts + tmm - 1) // tmm) * tmm
    ends = jnp.cumsum(padded)
    offs = ends - padded
    pos = offs[e] + rank
    n_tiles = (2 * t + N_EXPERTS * (tmm - 1)) // tmm + 1
    n_valid = (ends[-1] // tmm).astype(jnp.int32).reshape(1)
    tile_expert = jnp.minimum(jnp.searchsorted(ends, jnp.arange(n_tiles, dtype=jnp.int32) * tmm, side='right'),
                              N_EXPERTS - 1).astype(jnp.int32)
    xs = moe_dispatch(h.reshape(t, d), pos, n_tiles * tmm, tm=tm)
    ys = moe_experts(xs, tile_expert, n_valid, w1.astype(BF16), w3.astype(BF16), w2.astype(BF16), tmm=tmm)
    wts = w.transpose(0, 2, 1).reshape(t, 2)
    return moe_combine(x, gate, wts, pos, ys, tm=tm, n_ctx_tiles=n_ctx_tiles)


FFT_N1 = 64
FFT_N2 = 128
FFT_N = FFT_N1 * FFT_N2
FFT_K2 = FFT_N2 // 2 + 1
FFT_K2_PAD = 72
FFT_COLS = 8
HY_IN_TILE = 512
HY_FILTER_TILE = 256


def dft_constants():
    n1 = np.arange(FFT_N1)
    n2 = np.arange(FFT_N2)
    k2 = np.arange(FFT_K2_PAD)
    n = n1[:, None, None] + FFT_N1 * n2[None, None, :]
    ang = 2.0 * np.pi * ((k2[None, :, None] * n) % FFT_N) / FFT_N
    live = (k2 < FFT_K2)[None, :, None]
    f1 = np.concatenate([np.cos(ang) * live, -np.sin(ang) * live], axis=1)
    wgt = np.where((k2 == 0) | (k2 == FFT_N2 // 2), 1.0, 2.0) * (k2 < FFT_K2) / FFT_N
    ang3 = np.transpose(ang, (0, 2, 1))
    f3 = np.concatenate([np.cos(ang3) * wgt, -np.sin(ang3) * wgt], axis=2)
    f3 = f3[:, :FFT_N2 // 2]
    a = 2.0 * np.pi * ((n1[:, None] * n1[None, :]) % FFT_N1) / FFT_N1
    cr, ci = np.cos(a), -np.sin(a)
    m_fwd = np.block([[cr, -ci], [ci, cr]])
    m_inv = np.block([[cr, ci], [-ci, cr]])
    return (jnp.asarray(f1, BF16), jnp.asarray(f3, BF16), jnp.asarray(m_fwd, BF16), jnp.asarray(m_inv, BF16))


def _fft_stage1_kernel(x_ref, f_ref, o_ref, *, width):
    for g in range(FFT_COLS):
        cols = slice(g * width, (g + 1) * width)
        o_ref[:, cols] = _dot(f_ref[g], x_ref[:, cols].astype(BF16)).astype(o_ref.dtype)


def fft_stage1(xv, f1, *, width):
    nb, k, cols = xv.shape
    blk = FFT_COLS * width
    return pl.pallas_call(
        functools.partial(_fft_stage1_kernel, width=width),
        grid=(nb, FFT_N1 // FFT_COLS),
        in_specs=[pl.BlockSpec((None, k, blk), lambda b, j: (b, 0, j)),
                  pl.BlockSpec((FFT_COLS, 2 * FFT_K2_PAD, k), lambda b, j: (j, 0, 0))],
        out_specs=pl.BlockSpec((None, 2 * FFT_K2_PAD, blk), lambda b, j: (b, 0, j)),
        out_shape=jax.ShapeDtypeStruct((nb, 2 * FFT_K2_PAD, cols), BF16),
        compiler_params=_SEQ2,
        name="fft_stage1",
    )(xv, f1[:, :, :k])


def _fft_mid_kernel(a_ref, h_ref, mf_ref, mi_ref, o_ref):
    for kk in range(SUBLANES):
        a = jnp.concatenate([a_ref[0, kk], a_ref[1, kk]], axis=0)
        x = _dot(mf_ref[...], a)
        xr, xi = x[:FFT_N1], x[FFT_N1:]
        hr, hi = h_ref[kk, :FFT_N1], h_ref[kk, FFT_N1:]
        y = jnp.concatenate([xr * hr - xi * hi, xr * hi + xi * hr], axis=0).astype(BF16)
        b = _dot(mi_ref[...], y)
        o_ref[0, kk] = b[:FFT_N1].astype(o_ref.dtype)
        o_ref[1, kk] = b[FFT_N1:].astype(o_ref.dtype)


def fft_mid(a5, h, m_fwd, m_inv):
    nb, _, k2p, n1, c = a5.shape
    blk = pl.BlockSpec((None, 2, SUBLANES, n1, c), lambda j, b: (b, 0, j, 0, 0))
    return pl.pallas_call(
        _fft_mid_kernel,
        grid=(k2p // SUBLANES, nb),
        in_specs=[blk, pl.BlockSpec((SUBLANES, 2 * n1, c), lambda j, b: (j, 0, 0)),
                  pl.BlockSpec(m_fwd.shape, lambda j, b: (0, 0)), pl.BlockSpec(m_inv.shape, lambda j, b: (0, 0))],
        out_specs=blk,
        out_shape=jax.ShapeDtypeStruct(a5.shape, BF16),
        compiler_params=_SEQ2,
        name="fft_mid",
    )(a5, h, m_fwd, m_inv)


def _fft_filter_mid_kernel(a_ref, mf_ref, o_ref):
    for kk in range(SUBLANES):
        a = jnp.concatenate([a_ref[0, kk], a_ref[1, kk]], axis=0)
        o_ref[kk] = _dot(mf_ref[...], a)


def fft_filter_mid(a5, m_fwd):
    no, _, k2p, n1, c = a5.shape
    return pl.pallas_call(
        _fft_filter_mid_kernel,
        grid=(no, k2p // SUBLANES),
        in_specs=[pl.BlockSpec((None, 2, SUBLANES, n1, c), lambda o, j: (o, 0, j, 0, 0)),
                  pl.BlockSpec(m_fwd.shape, lambda o, j: (0, 0))],
        out_specs=pl.BlockSpec((None, SUBLANES, 2 * n1, c), lambda o, j: (o, j, 0, 0)),
        out_shape=jax.ShapeDtypeStruct((no, k2p, 2 * n1, c), F32),
        compiler_params=_SEQ2,
        name="fft_filter_mid",
    )(a5, m_fwd)


def _fft_stage3_kernel(b_ref, f_ref, u_ref, g_ref, skip_ref, o_ref, *, width):
    for g in range(FFT_COLS):
        cols = slice(g * width, (g + 1) * width)
        y = _dot(f_ref[g], b_ref[:, cols])
        o_ref[:, cols] = g_ref[:, cols] * (y + skip_ref[...] * u_ref[:, cols])


def fft_stage3(bv, f3, uv, gv, skip, *, width):
    nb, rows, cols = uv.shape
    blk = FFT_COLS * width
    view = pl.BlockSpec((None, rows, blk), lambda b, j: (b, 0, j))
    return pl.pallas_call(
        functools.partial(_fft_stage3_kernel, width=width),
        grid=(nb, FFT_N1 // FFT_COLS),
        in_specs=[pl.BlockSpec((None, 2 * FFT_K2_PAD, blk), lambda b, j: (b, 0, j)),
                  pl.BlockSpec((FFT_COLS, rows, 2 * FFT_K2_PAD), lambda b, j: (j, 0, 0)),
                  view, view, pl.BlockSpec((1, width), lambda b, j: (0, 0))],
        out_specs=view,
        out_shape=jax.ShapeDtypeStruct(uv.shape, F32),
        compiler_params=_SEQ2,
        name="fft_stage3",
    )(bv, f3, uv, gv, skip.reshape(1, width))


def long_conv(uv, gv, h, skip, consts, *, width):
    f1, f3, m_fwd, m_inv = consts
    nb = uv.shape[0]
    a = fft_stage1(uv, f1, width=width)
    a5 = a.reshape(nb, 2, FFT_K2_PAD, FFT_N1, width)
    b5 = fft_mid(a5, h, m_fwd, m_inv)
    bv = b5.reshape(nb, 2 * FFT_K2_PAD, FFT_N1 * width)
    return fft_stage3(bv, f3, uv, gv, skip, width=width)


def filter_features(n):
    t = jnp.linspace(0.0, 1.0, n, dtype=F32)[:, None]
    w = (2.0 * math.pi / n) * jnp.arange(n, dtype=F32)[:, None]
    f = jnp.linspace(1e-4, HY_BANDS - 1, HY_BANDS, dtype=F32)[None, :]
    z = jnp.concatenate([t, jnp.cos(f * w), -jnp.sin(f * w)], axis=-1)
    z2 = jnp.concatenate([z, z[:1], z[:0:-1]], axis=0)
    return jnp.pad(z2, ((0, 0), (0, LANES - z2.shape[1])))


def _filter_mlp_kernel(z_ref, w1_ref, b1_ref, f1_ref, w2_ref, b2_ref, f2_ref, o_ref):
    a = jnp.sin(f1_ref[...] * (_dot(z_ref[...].astype(BF16), w1_ref[...].astype(BF16)) + b1_ref[...]))
    o_ref[...] = jnp.sin(f2_ref[...] * (_dot(a.astype(BF16), w2_ref[...].astype(BF16)) + b2_ref[...]))


def filter_mlp(z2, w1, b1, f1, w2, b2, f2, *, tr):
    rows = z2.shape[0]
    hid = w2.shape[0]
    w1p = jnp.pad(w1, ((0, LANES - w1.shape[0]), (0, 0)))
    vec = lambda v: v.reshape(1, hid)
    const = lambda i: (0, 0)
    return pl.pallas_call(
        _filter_mlp_kernel,
        grid=(rows // tr,),
        in_specs=[pl.BlockSpec((tr, LANES), lambda i: (i, 0)), pl.BlockSpec((LANES, hid), const),
                  pl.BlockSpec((1, hid), const), pl.BlockSpec((1, hid), const), pl.BlockSpec((hid, hid), const),
                  pl.BlockSpec((1, hid), const), pl.BlockSpec((1, hid), const)],
        out_specs=pl.BlockSpec((tr, hid), lambda i: (i, 0)),
        out_shape=jax.ShapeDtypeStruct((rows, hid), F32),
        name="filter_mlp",
    )(z2, w1p, vec(b1), vec(f1), w2, vec(b2), vec(f2))


def _filter_kernel(a_ref, t_ref, wf_ref, wb_ref, df_ref, db_ref, o_ref):
    n = a_ref.shape[0] // 2
    a = a_ref[...].astype(BF16)
    t = t_ref[...]
    hf = _dot(a[:n], wf_ref[...].astype(BF16)) * jnp.exp(-t[:n] * jnp.abs(df_ref[...]))
    hb = _dot(a[n:], wb_ref[...].astype(BF16)) * jnp.exp(-t[n:] * jnp.abs(db_ref[...]))
    row = lax.broadcasted_iota(jnp.int32, hb.shape, 0)
    hb0 = hb[0:1]
    hb = jnp.where(row == 0, 0.0, hb)
    norm = (jnp.sum(jnp.abs(hf), axis=0, keepdims=True) + jnp.sum(jnp.abs(hb), axis=0, keepdims=True)
            + jnp.abs(hb0) + 1e-6)
    hf = jnp.where(row == 0, hf + hb0, hf)
    o_ref[0:n, :] = hf / norm
    o_ref[n:, :] = hb / norm


def hyena_kernels(a2, z2, w3, decay, *, width, tc):
    rows, hid = a2.shape
    w3r = w3.reshape(hid, HY_ORDER, 2, width).transpose(1, 2, 0, 3)
    dec = decay.reshape(HY_ORDER, 2, 1, width)
    tcol = z2[:, 0:1]
    wspec = lambda s: pl.BlockSpec((None, None, hid, tc), lambda o, c: (o, s, 0, c))
    dspec = lambda s: pl.BlockSpec((None, None, 1, tc), lambda o, c: (o, s, 0, c))
    return pl.pallas_call(
        _filter_kernel,
        grid=(HY_ORDER, width // tc),
        in_specs=[pl.BlockSpec((rows, hid), lambda o, c: (0, 0)), pl.BlockSpec((rows, 1), lambda o, c: (0, 0)),
                  wspec(0), wspec(1), dspec(0), dspec(1)],
        out_specs=pl.BlockSpec((None, rows, tc), lambda o, c: (o, 0, c)),
        out_shape=jax.ShapeDtypeStruct((HY_ORDER, rows, width), F32),
        compiler_params=_SEQ2,
        name="hyena_filters",
    )(a2, tcol, w3r, w3r, dec, dec)


def filter_spectra(p, consts, *, n, width):
    f1, _, m_fwd, _ = consts
    z2 = filter_features(n)
    a2 = filter_mlp(z2, p['w1'], p['b1'], p['f1'], p['w2'], p['b2'], p['f2'], tr=1024)
    kc = hyena_kernels(a2, z2, p['w3'], p['decay'], width=width, tc=min(HY_FILTER_TILE, width))
    kv = kc.reshape(HY_ORDER, FFT_N2, FFT_N1 * width)
    a = fft_stage1(kv, f1, width=width)
    return fft_filter_mid(a.reshape(HY_ORDER, 2, FFT_K2_PAD, FFT_N1, width), m_fwd)


def _hy_in_kernel(x_ref, xp_ref, xn_ref, sh_ref, sc_ref, g_ref, w_ref, cw_ref, cb_ref, v_ref, x1_ref, x2_ref):
    j = pl.program_id(1)
    nt = pl.num_programs(1)
    tm = x_ref.shape[0]
    width = v_ref.shape[-1]
    mod = lambda x: (_rms(x, g_ref[...]) * (1.0 + sc_ref[...]) + sh_ref[...]).astype(BF16)
    h = mod(x_ref[...])
    h_prev = mod(xp_ref[...])
    h_next = mod(xn_ref[...])
    keep_prev = jnp.where(j > 0, 1.0, 0.0)
    keep_next = jnp.where(j < nt - 1, 1.0, 0.0)
    row = lax.broadcasted_iota(jnp.int32, (tm, width), 0)
    cw = cw_ref[...]
    for part, o_ref in enumerate((v_ref, x1_ref, x2_ref)):
        cols = slice(part * width, (part + 1) * width)
        w = w_ref[:, cols]
        p = _dot(h, w)
        p_prev = _dot(h_prev, w)[SUBLANES - 1:SUBLANES] * keep_prev
        p_next = _dot(h_next, w)[0:1] * keep_next
        before = jnp.where(row == 0, p_prev, pltpu.roll(p, 1, 0))
        after = jnp.where(row == tm - 1, p_next, pltpu.roll(p, tm - 1, 0))
        o_ref[...] = cw[0:1, cols] * before + cw[1:2, cols] * p + cw[2:3, cols] * after + cb_ref[:, cols]


def hy_in_proj(x, shift, scale, norm_g, w_in, conv_w, conv_b, *, tm):
    nb, n, d = x.shape
    width = w_in.shape[1] // 3
    rb = tm // SUBLANES
    last = n // SUBLANES - 1
    tok = lambda b, j: (b, j, 0)
    mod_map = lambda b, j: (b, 0, 0)
    const = lambda b, j: (0, 0)
    out = pl.BlockSpec((None, tm, width), tok)
    return pl.pallas_call(
        _hy_in_kernel,
        grid=(nb, n // tm),
        in_specs=[pl.BlockSpec((None, tm, d), tok),
                  pl.BlockSpec((None, SUBLANES, d), lambda b, j: (b, jnp.maximum(j * rb - 1, 0), 0)),
                  pl.BlockSpec((None, SUBLANES, d), lambda b, j: (b, jnp.minimum((j + 1) * rb, last), 0)),
                  pl.BlockSpec((None, 1, d), mod_map), pl.BlockSpec((None, 1, d), mod_map),
                  pl.BlockSpec((1, d), const), pl.BlockSpec(w_in.shape, const),
                  pl.BlockSpec(conv_w.shape, const), pl.BlockSpec((1, 3 * width), const)],
        out_specs=[out, out, out],
        out_shape=[jax.ShapeDtypeStruct((nb, n, width), F32)] * 3,
        compiler_params=_SEQ2,
        name="hy_in_proj",
    )(x, x, x, shift, scale, norm_g.reshape(1, d), w_in.astype(BF16), conv_w, conv_b.reshape(1, -1))


def _hy_out_kernel(x_ref, gate_ref, z_ref, w_ref, o_ref):
    o_ref[...] = x_ref[...] + gate_ref[...] * _dot(z_ref[...].astype(BF16), w_ref[...])


def hy_out_proj(x, gate, z, w_out, *, tm):
    nb, n, d = x.shape
    tok = lambda b, j: (b, j, 0)
    return pl.pallas_call(
        _hy_out_kernel,
        grid=(nb, n // tm),
        in_specs=[pl.BlockSpec((None, tm, d), tok), pl.BlockSpec((None, 1, d), lambda b, j: (b, 0, 0)),
                  pl.BlockSpec((None, tm, z.shape[-1]), tok), pl.BlockSpec(w_out.shape, lambda b, j: (0, 0))],
        out_specs=pl.BlockSpec((None, tm, d), tok),
        out_shape=jax.ShapeDtypeStruct(x.shape, F32),
        compiler_params=_SEQ2,
        name="hy_out_proj",
    )(x, gate, z, w_out.astype(BF16))


def hyena_mixer(x, shift, scale, gate, norm_g, p):
    nb, n, d = x.shape
    width = p['w_in'].shape[1] // 3
    assert 2 * n == FFT_N
    consts = dft_constants()
    h = filter_spectra(p, consts, n=n, width=width)
    v, x1, x2 = hy_in_proj(x, shift, scale, norm_g, p['w_in'], p['conv_w'], p['conv_b'], tm=HY_IN_TILE)
    view = lambda a: a.reshape(nb, FFT_N2 // 2, FFT_N1 * width)
    z = long_conv(view(v), view(x1), h[0], p['skip'][0], consts, width=width)
    z = long_conv(z, view(x2), h[1], p['skip'][1], consts, width=width)
    return hy_out_proj(x, gate, z.reshape(nb, n, width), p['w_out'], tm=HY_IN_TILE)


def _final_norm_kernel(x_ref, g_ref, o_ref):
    o_ref[...] = _rms(x_ref[...], g_ref[...])


def final_norm(x, g):
    b, n, d = x.shape
    tm = 1024
    out = pl.pallas_call(
        _final_norm_kernel,
        grid=(b * n // tm,),
        in_specs=[pl.BlockSpec((tm, d), lambda i: (i, 0)), pl.BlockSpec((1, d), lambda i: (0, 0))],
        out_specs=pl.BlockSpec((tm, d), lambda i: (i, 0)),
        out_shape=jax.ShapeDtypeStruct((b * n, d), x.dtype),
        name="final_norm",
    )(x.reshape(b * n, d), g.reshape(1, d))
    return out.reshape(b, n, d)


def kernel(x, c, ctx, c_ctx, ada_w, ada_b, norm1_g, norm2_g, final_g, ev_w_in, mla_q_norm_g, mla_kv_norm_g, mla_w_uq, mla_w_ukv, lru_conv_w, lru_conv_b, lru_w_a, lru_b_a, lru_w_x, lru_b_x, lru_lambda, ev_w_out, od_w_in, hy_conv_w, hy_conv_b, hy_w1, hy_b1, hy_freq1, hy_w2, hy_b2, hy_freq2, hy_w3, hy_decay, hy_skip, od_w_out, router_w, router_b, moe_w1, moe_w3, moe_w2):
    nb, n_lat, d = x.shape
    n_ctx = ctx.shape[1]
    n_ctx_tiles = n_ctx // TOKEN_TILE
    cond = jnp.concatenate([jax.nn.silu(c), jax.nn.silu(c_ctx)[None, :]], axis=0)
    cond = jnp.pad(cond, ((0, 2 * SUBLANES - nb - 1), (0, 0)))

    def mod_rows(layer):
        mod = rows_matmul(cond, ada_w[layer], ada_b[layer], tn=D_MODEL)[:nb + 1]
        return [m[:, None, :] for m in jnp.split(mod, 6, axis=-1)]

    sh1, sc1, g1, sh2, sc2, g2 = mod_rows(0)
    x_all = jnp.concatenate([x, ctx], axis=1)
    p0 = dict(w_in=ev_w_in[0], q_norm_g=mla_q_norm_g[0], kv_norm_g=mla_kv_norm_g[0], w_uq=mla_w_uq[0],
              w_ukv=mla_w_ukv[0], conv_w=lru_conv_w[0], conv_b=lru_conv_b[0], w_a=lru_w_a[0], b_a=lru_b_a[0],
              w_x=lru_w_x[0], b_x=lru_b_x[0], lam=lru_lambda[0], w_out=ev_w_out[0])
    x_all = even_mixer(x_all, sh1, sc1, g1, norm1_g[0], p0, n_lat=n_lat, n_ctx=n_ctx)
    x_all = sparse_moe(x_all, sh2, sc2, g2, norm2_g[0], router_w, router_b, moe_w1[0], moe_w3[0], moe_w2[0],
                       n_ctx_tiles=n_ctx_tiles)
    x = x_all[:, :n_lat]

    sh1, sc1, g1, sh2, sc2, g2 = [m[:nb] for m in mod_rows(1)]
    p1 = dict(w_in=od_w_in[0], conv_w=hy_conv_w[0], conv_b=hy_conv_b[0], w1=hy_w1[0], b1=hy_b1[0], f1=hy_freq1[0],
              w2=hy_w2[0], b2=hy_b2[0], f2=hy_freq2[0], w3=hy_w3[0], decay=hy_decay[0], skip=hy_skip[0],
              w_out=od_w_out[0])
    x = hyena_mixer(x, sh1, sc1, g1, norm1_g[1], p1)
    x = sparse_moe(x, sh2, sc2, g2, norm2_g[1], router_w, router_b, moe_w1[1], moe_w3[1], moe_w2[1], n_ctx_tiles=0)
    return final_norm(x, final_g)
```

```python
import functools
import math

import jax
import jax.numpy as jnp
import numpy as np
from jax import lax
from jax.experimental import pallas as pl
from jax.experimental.pallas import tpu as pltpu

D_MODEL = 1024
DEPTH = 2
GRID_W = 64
RMS_EPS = 1e-6

MLA_HEADS = 8
MLA_Q_LORA = 384
MLA_KV_LORA = 256
MLA_NOPE = 64
MLA_ROPE = 32
MLA_V = 64
MLA_QK = MLA_NOPE + MLA_ROPE
ROPE_PAIRS = MLA_ROPE // 4
ROPE_BASE = 10000.0

LRU_WIDTH = 512
LRU_BLOCKS = 8
LRU_BLOCK_DIM = LRU_WIDTH // LRU_BLOCKS
LRU_C = 8.0
LRU_CONV = 4

HY_WIDTH = D_MODEL
HY_ORDER = 2
HY_BANDS = 16
HY_HIDDEN = 64

N_EXPERTS = 16
N_GROUPS = 4
EXPERTS_PER_GROUP = N_EXPERTS // N_GROUPS

LANES = 128
SUBLANES = 8
HEAD_PAD = LANES

TOKEN_TILE = 256
MOE_ROW_TILE = 512
ATTN_Q_TILE = 512
ATTN_KV_CHUNK = 1024
LRU_CHUNK = 256

BF16 = jnp.bfloat16
F32 = jnp.float32
_HI = lax.Precision.HIGHEST
_NEG_INF = float('-inf')


def _dot(a, b):
    return jnp.dot(a, b, preferred_element_type=F32)


def _rms(x, g):
    return x * lax.rsqrt(jnp.mean(x * x, axis=-1, keepdims=True) + RMS_EPS) * g


def _mod_map(nb, nt, n_ctx_tiles):
    if n_ctx_tiles:
        return lambda b, j: (jnp.where(j >= nt - n_ctx_tiles, nb, b), 0, 0)
    return lambda b, j: (b, 0, 0)


_SEQ2 = pltpu.CompilerParams(dimension_semantics=("arbitrary", "arbitrary"))


def _rows_matmul_kernel(x_ref, w_ref, b_ref, o_ref):
    o_ref[...] = _dot(x_ref[...].astype(BF16), w_ref[...].astype(BF16)) + b_ref[...]


def rows_matmul(x, w, b, *, tn):
    m, k = x.shape
    n = w.shape[1]
    return pl.pallas_call(
        _rows_matmul_kernel,
        grid=(n // tn,),
        in_specs=[pl.BlockSpec((m, k), lambda j: (0, 0)),
                  pl.BlockSpec((k, tn), lambda j: (0, j)),
                  pl.BlockSpec((1, tn), lambda j: (0, j))],
        out_specs=pl.BlockSpec((m, tn), lambda j: (0, j)),
        out_shape=jax.ShapeDtypeStruct((m, n), F32),
        name="rows_matmul",
    )(x, w, b.reshape(1, n))


def _rot_cols(w):
    p = ROPE_PAIRS
    return jnp.concatenate([-w[:, p:2 * p], w[:, 0:p], -w[:, 3 * p:4 * p], w[:, 2 * p:3 * p]], axis=1)


def _head_pad_cols(w, width):
    k = w.shape[0]
    w = w.reshape(k, MLA_HEADS, width)
    return jnp.pad(w, ((0, 0), (0, 0), (0, HEAD_PAD - width))).reshape(k, MLA_HEADS * HEAD_PAD)


def prep_even_weights(w_in, w_uq, w_ukv, w_out):
    d = w_in.shape[0]
    s1 = MLA_Q_LORA + MLA_KV_LORA
    s2 = s1 + MLA_ROPE
    w_kr = w_in[:, s1:s2]
    place = lambda w: jnp.pad(w, ((0, 0), (MLA_NOPE, HEAD_PAD - MLA_QK)))
    w_big = jnp.concatenate([w_in[:, :s1], w_in[:, s2:], place(w_kr), place(_rot_cols(w_kr))], axis=1).astype(BF16)
    uq = w_uq.reshape(MLA_Q_LORA, MLA_HEADS, MLA_QK)
    wq_a = _head_pad_cols(w_uq, MLA_QK)
    uq_rot = jnp.stack([_rot_cols(uq[:, h, MLA_NOPE:]) for h in range(MLA_HEADS)], axis=1)
    wq_b = jnp.pad(uq_rot, ((0, 0), (0, 0), (MLA_NOPE, HEAD_PAD - MLA_QK))).reshape(MLA_Q_LORA, MLA_HEADS * HEAD_PAD)
    ukv = w_ukv.reshape(MLA_KV_LORA, MLA_HEADS, MLA_NOPE + MLA_V)
    wk = _head_pad_cols(ukv[:, :, :MLA_NOPE].reshape(MLA_KV_LORA, -1), MLA_NOPE)
    wv = _head_pad_cols(ukv[:, :, MLA_NOPE:].reshape(MLA_KV_LORA, -1), MLA_V)
    w_kv = jnp.concatenate([wk, wv], axis=1)
    att_rows = MLA_HEADS * MLA_V
    wo_att = jnp.pad(w_out[:att_rows].reshape(MLA_HEADS, MLA_V, d),
                     ((0, 0), (0, HEAD_PAD - MLA_V), (0, 0))).reshape(MLA_HEADS * HEAD_PAD, d)
    return dict(w_big=w_big, wq_a=wq_a.astype(BF16), wq_b=wq_b.astype(BF16), w_kv=w_kv.astype(BF16),
                wo_att=wo_att.astype(BF16), wo_lru=w_out[att_rows:].astype(BF16))


def rope_tables(n_lat, n_ctx):
    rows = n_lat // GRID_W
    row = jnp.repeat(jnp.arange(rows), GRID_W)
    col = jnp.tile(jnp.arange(GRID_W), rows)
    inv_freq = ROPE_BASE ** (-jnp.arange(ROPE_PAIRS, dtype=F32) / ROPE_PAIRS)
    ang = jnp.stack([row, col], axis=-1).astype(F32)[:, :, None] * inv_freq
    cos, sin = jnp.cos(ang), jnp.sin(ang)
    c32 = jnp.concatenate([cos[:, 0], cos[:, 0], cos[:, 1], cos[:, 1]], axis=-1)
    s32 = jnp.concatenate([sin[:, 0], sin[:, 0], sin[:, 1], sin[:, 1]], axis=-1)
    c32 = jnp.concatenate([c32, jnp.ones((n_ctx, MLA_ROPE), F32)], axis=0)
    s32 = jnp.concatenate([s32, jnp.zeros((n_ctx, MLA_ROPE), F32)], axis=0)
    n = n_lat + n_ctx
    pad_hi = jnp.zeros((n, HEAD_PAD - MLA_QK), F32)
    scale = MLA_QK ** -0.5
    cq = jnp.concatenate([jnp.full((n, MLA_NOPE), scale, F32), c32 * scale, pad_hi], axis=-1)
    sq = jnp.concatenate([jnp.zeros((n, MLA_NOPE), F32), s32 * scale, pad_hi], axis=-1)
    ck = jnp.concatenate([jnp.zeros((n, MLA_NOPE), F32), c32, pad_hi], axis=-1)
    sk = jnp.concatenate([jnp.zeros((n, MLA_NOPE), F32), s32, pad_hi], axis=-1)
    return cq, sq, ck, sk


def _even_in_kernel(x_ref, sh_ref, sc_ref, g_ref, wbig_ref, qg_ref, kvg_ref, wqa_ref, wqb_ref, wkv_ref,
                    cq_ref, sq_ref, ck_ref, sk_ref, q_ref, k_ref, v_ref, ux_ref, ug_ref):
    h = (_rms(x_ref[...], g_ref[...]) * (1.0 + sc_ref[...]) + sh_ref[...]).astype(BF16)
    p = _dot(h, wbig_ref[...])
    o1 = MLA_Q_LORA
    o2 = o1 + MLA_KV_LORA
    o3 = o2 + LRU_WIDTH
    o4 = o3 + LRU_WIDTH
    o5 = o4 + HEAD_PAD
    ux_ref[...] = p[:, o2:o3]
    ug_ref[...] = p[:, o3:o4]
    cqn = _rms(p[:, :o1], qg_ref[...]).astype(BF16)
    rep = lambda t: jnp.concatenate([t] * MLA_HEADS, axis=-1)
    q = _dot(cqn, wqa_ref[...]) * rep(cq_ref[...]) + _dot(cqn, wqb_ref[...]) * rep(sq_ref[...])
    q_ref[...] = q.astype(BF16)
    ckvn = _rms(p[:, o1:o2], kvg_ref[...]).astype(BF16)
    kv = _dot(ckvn, wkv_ref[...])
    k_rope = p[:, o4:o5] * ck_ref[...] + p[:, o5:] * sk_ref[...]
    hw = MLA_HEADS * HEAD_PAD
    k_ref[...] = (kv[:, :hw] + rep(k_rope)).astype(BF16)
    lane = lax.broadcasted_iota(jnp.int32, (1, hw), 1)
    ones_col = jnp.where((lane & (HEAD_PAD - 1)) == MLA_V, 1.0, 0.0)
    v_ref[...] = (kv[:, hw:] + ones_col).astype(BF16)


def even_in_proj(x, shift, scale, norm_g, wts, q_norm_g, kv_norm_g, tables, *, tm, n_ctx_tiles):
    nb, s_len, d = x.shape
    nt = s_len // tm
    hw = MLA_HEADS * HEAD_PAD
    mod_map = _mod_map(nb, nt, n_ctx_tiles)
    const = lambda b, j: (0, 0)
    tok = lambda b, j: (b, j, 0)
    tab = lambda b, j: (j, 0)
    full = lambda a: pl.BlockSpec(a.shape, const)
    row = lambda v: v.reshape(1, -1)
    args = [x, shift, scale, row(norm_g), wts['w_big'], row(q_norm_g), row(kv_norm_g),
            wts['wq_a'], wts['wq_b'], wts['w_kv']]
    in_specs = [pl.BlockSpec((None, tm, d), tok), pl.BlockSpec((None, 1, d), mod_map),
                pl.BlockSpec((None, 1, d), mod_map)]
    in_specs += [full(a) for a in args[3:]]
    in_specs += [pl.BlockSpec((tm, HEAD_PAD), tab)] * 4
    out_dims = [(hw, BF16), (hw, BF16), (hw, BF16), (LRU_WIDTH, F32), (LRU_WIDTH, F32)]
    return pl.pallas_call(
        _even_in_kernel,
        grid=(nb, nt),
        in_specs=in_specs,
        out_specs=[pl.BlockSpec((None, tm, w), tok) for w, _ in out_dims],
        out_shape=[jax.ShapeDtypeStruct((nb, s_len, w), dt) for w, dt in out_dims],
        compiler_params=_SEQ2,
        name="even_in_proj",
    )(*args, *tables)


def _attn_kernel(q_ref, k_ref, v_ref, o_ref):
    q = q_ref[...]
    tq = q.shape[0]
    n_k = k_ref.shape[0]
    m = jnp.full((tq, 1), _NEG_INF, F32)
    acc = jnp.zeros((tq, HEAD_PAD), F32)
    for start in range(0, n_k, ATTN_KV_CHUNK):
        size = min(ATTN_KV_CHUNK, n_k - start)
        s = lax.dot_general(q, k_ref[start:start + size, :], (((1,), (1,)), ((), ())), preferred_element_type=F32)
        m_new = jnp.maximum(m, jnp.max(s, axis=-1, keepdims=True))
        p = jnp.exp(s - m_new).astype(BF16)
        acc = jnp.exp(m - m_new) * acc + _dot(p, v_ref[start:start + size, :])
        m = m_new
    o_ref[...] = (acc / acc[:, MLA_V:MLA_V + 1]).astype(o_ref.dtype)


def attention(q, k, v, *, q_start, n_q, k_start, n_k, tq):
    nb = q.shape[0]
    qo = q_start // tq
    ko = k_start // n_k
    return pl.pallas_call(
        _attn_kernel,
        grid=(nb, MLA_HEADS, n_q // tq),
        in_specs=[pl.BlockSpec((None, tq, HEAD_PAD), lambda b, h, i: (b, qo + i, h)),
                  pl.BlockSpec((None, n_k, HEAD_PAD), lambda b, h, i: (b, ko, h)),
                  pl.BlockSpec((None, n_k, HEAD_PAD), lambda b, h, i: (b, ko, h))],
        out_specs=pl.BlockSpec((None, tq, HEAD_PAD), lambda b, h, i: (b, i, h)),
        out_shape=jax.ShapeDtypeStruct((nb, n_q, MLA_HEADS * HEAD_PAD), BF16),
        compiler_params=pltpu.CompilerParams(dimension_semantics=("arbitrary", "arbitrary", "arbitrary")),
        name="attention",
    )(q, k, v)


def _scan_block(a, b, h_prev, row, reverse):
    for s in (1, 2, 4):
        sh = SUBLANES - s if reverse else s
        a_s = pltpu.roll(a, sh, 0)
        b_s = pltpu.roll(b, sh, 0)
        ok = (row < SUBLANES - s) if reverse else (row >= s)
        b = jnp.where(ok, a * b_s + b, b)
        a = jnp.where(ok, a * a_s, a)
    return a * h_prev + b


def _rglru_kernel(ux_ref, ug_ref, cw_ref, cb_ref, wa_ref, ba_ref, wx_ref, bx_ref, lam_ref, o_ref,
                  xp_ref, a_ref, b_ref, hf_ref, hb_ref, *, n_lat, n_ctx):
    lanes = ux_ref.shape[-1]
    pad = SUBLANES
    zeros = jnp.zeros((pad, lanes), F32)
    lat0 = pad
    ctx0 = 2 * pad + n_lat
    xp_ref[0:pad, :] = zeros
    xp_ref[lat0 + n_lat:ctx0, :] = zeros
    xp_ref[ctx0 + n_ctx:ctx0 + n_ctx + pad, :] = zeros
    xp_ref[lat0:lat0 + n_lat, :] = ux_ref[0:n_lat, :]
    xp_ref[ctx0:ctx0 + n_ctx, :] = ux_ref[n_lat:n_lat + n_ctx, :]
    cw = cw_ref[...]
    n_chunks = (n_lat + n_ctx) // LRU_CHUNK
    n_lat_chunks = n_lat // LRU_CHUNK
    n_win = LRU_CHUNK + 2 * pad

    def coeffs(c, carry):
        src = pl.multiple_of(c * LRU_CHUNK + jnp.where(c >= n_lat_chunks, pad, 0), SUBLANES)
        dst = pl.multiple_of(c * LRU_CHUNK, SUBLANES)
        win = xp_ref[pl.ds(src, n_win), :]
        mid = lambda t: t[pad:pad + LRU_CHUNK]
        u = cb_ref[...] + cw[2:3, :] * mid(win)
        u = u + cw[0:1, :] * mid(pltpu.roll(win, 2, 0))
        u = u + cw[1:2, :] * mid(pltpu.roll(win, 1, 0))
        u = u + cw[3:4, :] * mid(pltpu.roll(win, n_win - 1, 0))
        ub = u.astype(BF16)
        for d in range(2):
            r = jax.nn.sigmoid(_dot(ub, wa_ref[d]) + ba_ref[d])
            i = jax.nn.sigmoid(_dot(ub, wx_ref[d]) + bx_ref[d])
            a = jnp.exp(lam_ref[d] * r)
            a_ref[d, pl.ds(dst, LRU_CHUNK), :] = a
            b_ref[d, pl.ds(dst, LRU_CHUNK), :] = jnp.sqrt(1.0 - a * a) * (i * u)
        return carry

    lax.fori_loop(0, n_chunks, coeffs, 0)
    row = lax.broadcasted_iota(jnp.int32, (SUBLANES, lanes), 0)

    def make_step(first_blk, n_blk):
        def step(t, carry):
            h_f, h_b = carry
            rf = pl.multiple_of((first_blk + t) * SUBLANES, SUBLANES)
            rb = pl.multiple_of((first_blk + n_blk - 1 - t) * SUBLANES, SUBLANES)
            out_f = _scan_block(a_ref[0, pl.ds(rf, SUBLANES), :], b_ref[0, pl.ds(rf, SUBLANES), :], h_f, row, False)
            out_b = _scan_block(a_ref[1, pl.ds(rb, SUBLANES), :], b_ref[1, pl.ds(rb, SUBLANES), :], h_b, row, True)
            hf_ref[pl.ds(rf, SUBLANES), :] = out_f
            hb_ref[pl.ds(rb, SUBLANES), :] = out_b
            h_f = jnp.broadcast_to(out_f[SUBLANES - 1:SUBLANES, :], out_f.shape)
            h_b = jnp.broadcast_to(out_b[0:1, :], out_b.shape)
            return h_f, h_b
        return step

    state = (jnp.zeros((SUBLANES, lanes), F32), jnp.zeros((SUBLANES, lanes), F32))
    state = lax.fori_loop(0, n_ctx // SUBLANES, make_step(n_lat // SUBLANES, n_ctx // SUBLANES), state)
    lax.fori_loop(0, n_lat // SUBLANES, make_step(0, n_lat // SUBLANES), state)

    def gate(c, carry):
        r0 = pl.multiple_of(c * LRU_CHUNK, SUBLANES)
        g = ug_ref[pl.ds(r0, LRU_CHUNK), :]
        gelu = 0.5 * g * (1.0 + jnp.tanh(math.sqrt(2.0 / math.pi) * (g + 0.044715 * (g * g * g))))
        y = hf_ref[pl.ds(r0, LRU_CHUNK), :] + hb_ref[pl.ds(r0, LRU_CHUNK), :]
        o_ref[pl.ds(r0, LRU_CHUNK), :] = (y * gelu).astype(o_ref.dtype)
        return carry

    lax.fori_loop(0, n_chunks, gate, 0)


def _block_diag(w):
    per = LANES // LRU_BLOCK_DIM
    nt = LRU_BLOCKS // per
    w = w.reshape(2, nt, per, LRU_BLOCK_DIM, LRU_BLOCK_DIM)
    eye = jnp.eye(per, dtype=w.dtype)
    out = jnp.einsum('dtpij,pq->dtpiqj', w, eye)
    return out.reshape(2, nt, LANES, LANES)


def rglru(ux, ug, conv_w, conv_b, w_a, b_a, w_x, b_x, lam, *, n_lat, n_ctx):
    nb, s_len, width = ux.shape
    nt = width // LANES
    wa = _block_diag(w_a).astype(BF16).transpose(1, 0, 2, 3)
    wx = _block_diag(w_x).astype(BF16).transpose(1, 0, 2, 3)
    lam_c = -LRU_C * jax.nn.softplus(-lam.astype(F32))
    lane3 = lambda v: v.reshape(2, 1, width)
    seq = pl.BlockSpec((None, s_len, LANES), lambda b, c: (b, 0, c))
    vec = lambda rows: pl.BlockSpec((rows, LANES), lambda b, c: (0, c))
    dvec = pl.BlockSpec((2, 1, LANES), lambda b, c: (0, 0, c))
    wspec = pl.BlockSpec((None, 2, LANES, LANES), lambda b, c: (c, 0, 0, 0))
    return pl.pallas_call(
        functools.partial(_rglru_kernel, n_lat=n_lat, n_ctx=n_ctx),
        grid=(nb, nt),
        in_specs=[seq, seq, vec(LRU_CONV), vec(1), wspec, dvec, wspec, dvec, dvec],
        out_specs=seq,
        out_shape=jax.ShapeDtypeStruct((nb, s_len, width), BF16),
        scratch_shapes=[pltpu.VMEM((s_len + 3 * SUBLANES, LANES), F32),
                        pltpu.VMEM((2, s_len, LANES), F32), pltpu.VMEM((2, s_len, LANES), F32),
                        pltpu.VMEM((s_len, LANES), F32), pltpu.VMEM((s_len, LANES), F32)],
        compiler_params=_SEQ2,
        name="rglru",
    )(ux, ug, conv_w, conv_b.reshape(1, width), wa, lane3(b_a), wx, lane3(b_x), lane3(lam_c))


def _even_out_kernel(x_ref, gate_ref, att_ref, lru_ref, wa_ref, wl_ref, o_ref):
    y = _dot(att_ref[...], wa_ref[...]) + _dot(lru_ref[...], wl_ref[...])
    o_ref[...] = x_ref[...] + gate_ref[...] * y


def even_out_proj(x, gate, att, lru, wo_att, wo_lru, *, tm, n_ctx_tiles):
    nb, s_len, d = x.shape
    nt = s_len // tm
    tok = lambda b, j: (b, j, 0)
    const = lambda b, j: (0, 0)
    return pl.pallas_call(
        _even_out_kernel,
        grid=(nb, nt),
        in_specs=[pl.BlockSpec((None, tm, d), tok), pl.BlockSpec((None, 1, d), _mod_map(nb, nt, n_ctx_tiles)),
                  pl.BlockSpec((None, tm, att.shape[-1]), tok), pl.BlockSpec((None, tm, lru.shape[-1]), tok),
                  pl.BlockSpec(wo_att.shape, const), pl.BlockSpec(wo_lru.shape, const)],
        out_specs=pl.BlockSpec((None, tm, d), tok),
        out_shape=jax.ShapeDtypeStruct((nb, s_len, d), F32),
        compiler_params=_SEQ2,
        name="even_out_proj",
    )(x, gate, att, lru, wo_att, wo_lru)


def even_mixer(x_all, shift, scale, gate, norm_g, p, *, n_lat, n_ctx):
    n_ctx_tiles = n_ctx // TOKEN_TILE
    tables = rope_tables(n_lat, n_ctx)
    wts = prep_even_weights(p['w_in'], p['w_uq'], p['w_ukv'], p['w_out'])
    q, k, v, ux, ug = even_in_proj(x_all, shift, scale, norm_g, wts, p['q_norm_g'], p['kv_norm_g'], tables,
                                   tm=TOKEN_TILE, n_ctx_tiles=n_ctx_tiles)
    att_l = attention(q, k, v, q_start=0, n_q=n_lat, k_start=0, n_k=n_lat + n_ctx, tq=ATTN_Q_TILE)
    att_c = attention(q, k, v, q_start=n_lat, n_q=n_ctx, k_start=n_lat, n_k=n_ctx, tq=n_ctx)
    att = jnp.concatenate([att_l, att_c], axis=1)
    lru = rglru(ux, ug, p['conv_w'], p['conv_b'], p['w_a'], p['b_a'], p['w_x'], p['b_x'], p['lam'],
                n_lat=n_lat, n_ctx=n_ctx)
    return even_out_proj(x_all, gate, att, lru, wts['wo_att'], wts['wo_lru'], tm=TOKEN_TILE, n_ctx_tiles=n_ctx_tiles)


def _first_argmax(vals):
    best = vals[0]
    idx = jnp.zeros(vals[0].shape, jnp.int32)
    for j in range(1, len(vals)):
        better = vals[j] > best
        idx = jnp.where(better, j, idx)
        best = jnp.where(better, vals[j], best)
    return idx, best


def _select(idx, vals):
    out = vals[0]
    for j in range(1, len(vals)):
        out = jnp.where(idx == j, vals[j], out)
    return out


def _moe_route_kernel(x_ref, sh_ref, sc_ref, g_ref, rwt_ref, rb_ref,
                      h_ref, e_ref, w_ref, rank_ref, cnt_ref, carry_ref, *, n_groups, per_group):
    first = jnp.logical_and(pl.program_id(0) == 0, pl.program_id(1) == 0)

    @pl.when(first)
    def _():
        carry_ref[...] = jnp.zeros_like(carry_ref)

    tm = x_ref.shape[0]
    n_exp = n_groups * per_group
    h = _rms(x_ref[...], g_ref[...]) * (1.0 + sc_ref[...]) + sh_ref[...]
    h_ref[...] = h
    logits = lax.dot_general(rwt_ref[...], h, (((1,), (1,)), ((), ())),
                             precision=_HI, preferred_element_type=F32)
    s = jax.nn.sigmoid(logits)
    sel = s + rb_ref[...]
    sel_rows = [sel[e:e + 1, :] for e in range(n_exp)]
    s_rows = [s[e:e + 1, :] for e in range(n_exp)]
    g_scores = []
    for g in range(n_groups):
        r = sel_rows[g * per_group:(g + 1) * per_group]
        pair_sums = [r[a] + r[b] for a in range(per_group) for b in range(a + 1, per_group)]
        g_scores.append(functools.reduce(jnp.maximum, pair_sums))
    g_idx, _ = _first_argmax(g_scores)
    v = [_select(g_idx, [sel_rows[g * per_group + j] for g in range(n_groups)]) for j in range(per_group)]
    sv = [_select(g_idx, [s_rows[g * per_group + j] for g in range(n_groups)]) for j in range(per_group)]
    i1, _ = _first_argmax(v)
    i2, _ = _first_argmax([jnp.full_like(v[0], _NEG_INF)]
                          + [jnp.where(i1 == j, _NEG_INF, v[j]) for j in range(per_group)])
    i2 = i2 - 1
    w1 = _select(i1, sv)
    w2 = _select(i2, sv)
    wsum = w1 + w2
    e1 = g_idx * per_group + i1
    e2 = g_idx * per_group + i2
    e_ref[0:1, :] = e1
    e_ref[1:2, :] = e2
    w_ref[0:1, :] = w1 / wsum
    w_ref[1:2, :] = w2 / wsum
    e_iota = lax.broadcasted_iota(jnp.int32, (n_exp, tm), 0)
    oh1 = e_iota == e1
    oh2 = e_iota == e2
    m = jnp.where(oh1, 1.0, jnp.where(oh2, 1.0, 0.0))
    r_i = lax.broadcasted_iota(jnp.int32, (tm, tm), 0)
    c_i = lax.broadcasted_iota(jnp.int32, (tm, tm), 1)
    upper = jnp.where(r_i < c_i, 1.0, 0.0).astype(BF16)
    pre = _dot(m.astype(BF16), upper)
    tot = pre + carry_ref[:, 0:1]
    rank_ref[0:1, :] = jnp.sum(jnp.where(oh1, tot, 0.0), axis=0, keepdims=True).astype(jnp.int32)
    rank_ref[1:2, :] = jnp.sum(jnp.where(oh2, tot, 0.0), axis=0, keepdims=True).astype(jnp.int32)
    new_carry = carry_ref[...] + jnp.sum(m, axis=1, keepdims=True)
    carry_ref[...] = new_carry
    cnt_ref[...] = new_carry


def moe_route(x, shift, scale, norm_g, router_w, router_b, *, tm, n_ctx_tiles, n_groups, per_group):
    nb, s_len, d = x.shape
    nt = s_len // tm
    n_exp = n_groups * per_group
    rwt = router_w.T
    rb = router_b.reshape(n_exp, 1)
    mod_map = _mod_map(nb, nt, n_ctx_tiles)
    tok_map = lambda b, j: (b * nt + j, 0, 0)
    return pl.pallas_call(
        functools.partial(_moe_route_kernel, n_groups=n_groups, per_group=per_group),
        grid=(nb, nt),
        in_specs=[
            pl.BlockSpec((None, tm, d), lambda b, j: (b, j, 0)),
            pl.BlockSpec((None, 1, d), mod_map),
            pl.BlockSpec((None, 1, d), mod_map),
            pl.BlockSpec((1, d), lambda b, j: (0, 0)),
            pl.BlockSpec((n_exp, d), lambda b, j: (0, 0)),
            pl.BlockSpec((n_exp, 1), lambda b, j: (0, 0)),
        ],
        out_specs=[
            pl.BlockSpec((None, tm, d), lambda b, j: (b, j, 0)),
            pl.BlockSpec((None, 2, tm), tok_map),
            pl.BlockSpec((None, 2, tm), tok_map),
            pl.BlockSpec((None, 2, tm), tok_map),
            pl.BlockSpec((n_exp, LANES), lambda b, j: (0, 0)),
        ],
        out_shape=[
            jax.ShapeDtypeStruct((nb, s_len, d), F32),
            jax.ShapeDtypeStruct((nb * nt, 2, tm), jnp.int32),
            jax.ShapeDtypeStruct((nb * nt, 2, tm), F32),
            jax.ShapeDtypeStruct((nb * nt, 2, tm), jnp.int32),
            jax.ShapeDtypeStruct((n_exp, LANES), F32),
        ],
        scratch_shapes=[pltpu.VMEM((n_exp, LANES), F32)],
        compiler_params=_SEQ2,
        name="moe_route",
    )(x, shift, scale, norm_g.reshape(1, d), rwt, rb)


def _row_copy(src_ref, src_row, dst_ref, dst_row, sem):
    return pltpu.make_async_copy(src_ref.at[pl.ds(src_row, 1)], dst_ref.at[pl.ds(dst_row, 1)], sem)


def _moe_dispatch_kernel(pos_ref, h_ref, xs_in_ref, xs_ref, sem):
    del xs_in_ref
    tm = h_ref.shape[0]

    def start(r, c):
        _row_copy(h_ref, r, xs_ref, pos_ref[0, r], sem).start()
        _row_copy(h_ref, r, xs_ref, pos_ref[1, r], sem).start(priority=1)
        return c

    lax.fori_loop(0, tm, start, 0)

    def wait(r, c):
        _row_copy(h_ref, 0, xs_ref, 0, sem).wait()
        _row_copy(h_ref, 0, xs_ref, 0, sem).wait()
        return c

    lax.fori_loop(0, tm, wait, 0)


def moe_dispatch(h2d, pos, n_rows, *, tm):
    t, d = h2d.shape
    zeros = jnp.zeros((n_rows, d), h2d.dtype)
    return pl.pallas_call(
        _moe_dispatch_kernel,
        grid=(t // tm,),
        in_specs=[
            pl.BlockSpec((None, 2, tm), lambda i: (i, 0, 0), memory_space=pltpu.SMEM),
            pl.BlockSpec((tm, d), lambda i: (i, 0)),
            pl.BlockSpec(memory_space=pl.ANY),
        ],
        out_specs=pl.BlockSpec(memory_space=pl.ANY),
        out_shape=jax.ShapeDtypeStruct((n_rows, d), h2d.dtype),
        scratch_shapes=[pltpu.SemaphoreType.DMA(())],
        input_output_aliases={2: 0},
        compiler_params=pltpu.CompilerParams(dimension_semantics=("arbitrary",), has_side_effects=True),
        name="moe_dispatch",
    )(pos, h2d, zeros)


def _moe_expert_kernel(te_ref, nv_ref, xs_ref, w1_ref, w3_ref, w2_ref, ys_ref):
    del te_ref

    @pl.when(pl.program_id(0) < nv_ref[0])
    def _():
        x = xs_ref[...].astype(BF16)
        a = _dot(x, w1_ref[...])
        b = _dot(x, w3_ref[...])
        act = (a * jax.nn.sigmoid(a) * b).astype(BF16)
        ys_ref[...] = _dot(act, w2_ref[...])

    @pl.when(pl.program_id(0) >= nv_ref[0])
    def _():
        ys_ref[...] = jnp.zeros_like(ys_ref)


def moe_experts(xs, tile_expert, n_valid, w1, w3, w2, *, tmm):
    n_rows, d = xs.shape
    n_tiles = n_rows // tmm
    ff = w1.shape[-1]
    row_map = lambda i, te, nv: (jnp.minimum(i, nv[0] - 1), 0)
    w_map = lambda i, te, nv: (te[jnp.minimum(i, nv[0] - 1)], 0, 0)
    grid_spec = pltpu.PrefetchScalarGridSpec(
        num_scalar_prefetch=2,
        grid=(n_tiles,),
        in_specs=[
            pl.BlockSpec((tmm, d), row_map),
            pl.BlockSpec((None, d, ff), w_map),
            pl.BlockSpec((None, d, ff), w_map),
            pl.BlockSpec((None, ff, d), w_map),
        ],
        out_specs=pl.BlockSpec((tmm, d), lambda i, te, nv: (i, 0)),
    )
    return pl.pallas_call(
        _moe_expert_kernel,
        grid_spec=grid_spec,
        out_shape=jax.ShapeDtypeStruct((n_rows, d), F32),
        compiler_params=pltpu.CompilerParams(dimension_semantics=("arbitrary",)),
        name="moe_experts",
    )(tile_expert, n_valid, xs, w1, w3, w2)


def _moe_combine_kernel(pos_ref, x_ref, gate_ref, w_ref, ys_ref, o_ref, buf_ref, sem):
    tm = x_ref.shape[0]

    def start(r, c):
        _row_copy(ys_ref, pos_ref[0, r], buf_ref.at[0], r, sem).start()
        _row_copy(ys_ref, pos_ref[1, r], buf_ref.at[1], r, sem).start(priority=1)
        return c

    lax.fori_loop(0, tm, start, 0)

    def wait(r, c):
        _row_copy(ys_ref, 0, buf_ref.at[0], 0, sem).wait()
        _row_copy(ys_ref, 0, buf_ref.at[1], 0, sem).wait()
        return c

    lax.fori_loop(0, tm, wait, 0)
    w = w_ref[...]
    y = w[:, 0:1] * buf_ref[0] + w[:, 1:2] * buf_ref[1]
    o_ref[...] = x_ref[...] + gate_ref[...] * y


def moe_combine(x, gate, wts, pos, ys, *, tm, n_ctx_tiles):
    nb, s_len, d = x.shape
    nt = s_len // tm
    return pl.pallas_call(
        _moe_combine_kernel,
        grid=(nb, nt),
        in_specs=[
            pl.BlockSpec((None, 2, tm), lambda b, j: (b * nt + j, 0, 0), memory_space=pltpu.SMEM),
            pl.BlockSpec((None, tm, d), lambda b, j: (b, j, 0)),
            pl.BlockSpec((None, 1, d), _mod_map(nb, nt, n_ctx_tiles)),
            pl.BlockSpec((tm, 2), lambda b, j: (b * nt + j, 0)),
            pl.BlockSpec(memory_space=pl.ANY),
        ],
        out_specs=pl.BlockSpec((None, tm, d), lambda b, j: (b, j, 0)),
        out_shape=jax.ShapeDtypeStruct((nb, s_len, d), F32),
        scratch_shapes=[pltpu.VMEM((2, tm, d), F32), pltpu.SemaphoreType.DMA(())],
        compiler_params=_SEQ2,
        name="moe_combine",
    )(pos, x, gate, wts, ys)


def sparse_moe(x, shift, scale, gate, norm_g, router_w, router_b, w1, w3, w2, *, n_ctx_tiles):
    nb, s_len, d = x.shape
    tm, tmm = TOKEN_TILE, MOE_ROW_TILE
    t = nb * s_len
    h, e, w, rank, cnt = moe_route(x, shift, scale, norm_g, router_w, router_b, tm=tm, n_ctx_tiles=n_ctx_tiles,
                                   n_groups=N_GROUPS, per_group=EXPERTS_PER_GROUP)
    counts = cnt[:, 0].astype(jnp.int32)
    padded = ((counts + tmm - 1) // tmm) * tmm
    ends = jnp.cumsum(padded)
    offs = ends - padded
    pos = offs[e] + rank
    n_tiles = (2 * t + N_EXPERTS * (tmm - 1)) // tmm + 1
    n_valid = (ends[-1] // tmm).astype(jnp.int32).reshape(1)
    tile_expert = jnp.minimum(jnp.searchsorted(ends, jnp.arange(n_tiles, dtype=jnp.int32) * tmm, side='right'),
                              N_EXPERTS - 1).astype(jnp.int32)
    xs = moe_dispatch(h.reshape(t, d), pos, n_tiles * tmm, tm=tm)
    ys = moe_experts(xs, tile_expert, n_valid, w1.astype(BF16), w3.astype(BF16), w2.astype(BF16), tmm=tmm)
    wts = w.transpose(0, 2, 1).reshape(t, 2)
    return moe_combine(x, gate, wts, pos, ys, tm=tm, n_ctx_tiles=n_ctx_tiles)


FFT_N1 = 64
FFT_N2 = 128
FFT_N = FFT_N1 * FFT_N2
FFT_K2 = FFT_N2 // 2 + 1
FFT_K2_PAD = 72
FFT_COLS = 8
HY_IN_TILE = 512
HY_FILTER_TILE = 256


def dft_constants():
    n1 = np.arange(FFT_N1)
    n2 = np.arange(FFT_N2)
    k2 = np.arange(FFT_K2_PAD)
    n = n1[:, None, None] + FFT_N1 * n2[None, None, :]
    ang = 2.0 * np.pi * ((k2[None, :, None] * n) % FFT_N) / FFT_N
    live = (k2 < FFT_K2)[None, :, None]
    f1 = np.concatenate([np.cos(ang) * live, -np.sin(ang) * live], axis=1)
    wgt = np.where((k2 == 0) | (k2 == FFT_N2 // 2), 1.0, 2.0) * (k2 < FFT_K2) / FFT_N
    ang3 = np.transpose(ang, (0, 2, 1))
    f3 = np.concatenate([np.cos(ang3) * wgt, -np.sin(ang3) * wgt], axis=2)
    f3 = f3[:, :FFT_N2 // 2]
    a = 2.0 * np.pi * ((n1[:, None] * n1[None, :]) % FFT_N1) / FFT_N1
    cr, ci = np.cos(a), -np.sin(a)
    m_fwd = np.block([[cr, -ci], [ci, cr]])
    m_inv = np.block([[cr, ci], [-ci, cr]])
    return (jnp.asarray(f1, BF16), jnp.asarray(f3, BF16), jnp.asarray(m_fwd, BF16), jnp.asarray(m_inv, BF16))


def _fft_stage1_kernel(x_ref, f_ref, o_ref, *, width):
    for g in range(FFT_COLS):
        cols = slice(g * width, (g + 1) * width)
        o_ref[:, cols] = _dot(f_ref[g], x_ref[:, cols].astype(BF16)).astype(o_ref.dtype)


def fft_stage1(xv, f1, *, width):
    nb, k, cols = xv.shape
    blk = FFT_COLS * width
    return pl.pallas_call(
        functools.partial(_fft_stage1_kernel, width=width),
        grid=(nb, FFT_N1 // FFT_COLS),
        in_specs=[pl.BlockSpec((None, k, blk), lambda b, j: (b, 0, j)),
                  pl.BlockSpec((FFT_COLS, 2 * FFT_K2_PAD, k), lambda b, j: (j, 0, 0))],
        out_specs=pl.BlockSpec((None, 2 * FFT_K2_PAD, blk), lambda b, j: (b, 0, j)),
        out_shape=jax.ShapeDtypeStruct((nb, 2 * FFT_K2_PAD, cols), BF16),
        compiler_params=_SEQ2,
        name="fft_stage1",
    )(xv, f1[:, :, :k])


def _fft_mid_kernel(a_ref, h_ref, mf_ref, mi_ref, o_ref):
    for kk in range(SUBLANES):
        a = jnp.concatenate([a_ref[0, kk], a_ref[1, kk]], axis=0)
        x = _dot(mf_ref[...], a)
        xr, xi = x[:FFT_N1], x[FFT_N1:]
        hr, hi = h_ref[kk, :FFT_N1], h_ref[kk, FFT_N1:]
        y = jnp.concatenate([xr * hr - xi * hi, xr * hi + xi * hr], axis=0).astype(BF16)
        b = _dot(mi_ref[...], y)
        o_ref[0, kk] = b[:FFT_N1].astype(o_ref.dtype)
        o_ref[1, kk] = b[FFT_N1:].astype(o_ref.dtype)


def fft_mid(a5, h, m_fwd, m_inv):
    nb, _, k2p, n1, c = a5.shape
    blk = pl.BlockSpec((None, 2, SUBLANES, n1, c), lambda j, b: (b, 0, j, 0, 0))
    return pl.pallas_call(
        _fft_mid_kernel,
        grid=(k2p // SUBLANES, nb),
        in_specs=[blk, pl.BlockSpec((SUBLANES, 2 * n1, c), lambda j, b: (j, 0, 0)),
                  pl.BlockSpec(m_fwd.shape, lambda j, b: (0, 0)), pl.BlockSpec(m_inv.shape, lambda j, b: (0, 0))],
        out_specs=blk,
        out_shape=jax.ShapeDtypeStruct(a5.shape, BF16),
        compiler_params=_SEQ2,
        name="fft_mid",
    )(a5, h, m_fwd, m_inv)


def _fft_filter_mid_kernel(a_ref, mf_ref, o_ref):
    for kk in range(SUBLANES):
        a = jnp.concatenate([a_ref[0, kk], a_ref[1, kk]], axis=0)
        o_ref[kk] = _dot(mf_ref[...], a)


def fft_filter_mid(a5, m_fwd):
    no, _, k2p, n1, c = a5.shape
    return pl.pallas_call(
        _fft_filter_mid_kernel,
        grid=(no, k2p // SUBLANES),
        in_specs=[pl.BlockSpec((None, 2, SUBLANES, n1, c), lambda o, j: (o, 0, j, 0, 0)),
                  pl.BlockSpec(m_fwd.shape, lambda o, j: (0, 0))],
        out_specs=pl.BlockSpec((None, SUBLANES, 2 * n1, c), lambda o, j: (o, j, 0, 0)),
        out_shape=jax.ShapeDtypeStruct((no, k2p, 2 * n1, c), F32),
        compiler_params=_SEQ2,
        name="fft_filter_mid",
    )(a5, m_fwd)


def _fft_stage3_kernel(b_ref, f_ref, u_ref, g_ref, skip_ref, o_ref, *, width):
    for g in range(FFT_COLS):
        cols = slice(g * width, (g + 1) * width)
        y = _dot(f_ref[g], b_ref[:, cols])
        o_ref[:, cols] = g_ref[:, cols] * (y + skip_ref[...] * u_ref[:, cols])


def fft_stage3(bv, f3, uv, gv, skip, *, width):
    nb, rows, cols = uv.shape
    blk = FFT_COLS * width
    view = pl.BlockSpec((None, rows, blk), lambda b, j: (b, 0, j))
    return pl.pallas_call(
        functools.partial(_fft_stage3_kernel, width=width),
        grid=(nb, FFT_N1 // FFT_COLS),
        in_specs=[pl.BlockSpec((None, 2 * FFT_K2_PAD, blk), lambda b, j: (b, 0, j)),
                  pl.BlockSpec((FFT_COLS, rows, 2 * FFT_K2_PAD), lambda b, j: (j, 0, 0)),
                  view, view, pl.BlockSpec((1, width), lambda b, j: (0, 0))],
        out_specs=view,
        out_shape=jax.ShapeDtypeStruct(uv.shape, F32),
        compiler_params=_SEQ2,
        name="fft_stage3",
    )(bv, f3, uv, gv, skip.reshape(1, width))


def long_conv(uv, gv, h, skip, consts, *, width):
    f1, f3, m_fwd, m_inv = consts
    nb = uv.shape[0]
    a = fft_stage1(uv, f1, width=width)
    a5 = a.reshape(nb, 2, FFT_K2_PAD, FFT_N1, width)
    b5 = fft_mid(a5, h, m_fwd, m_inv)
    bv = b5.reshape(nb, 2 * FFT_K2_PAD, FFT_N1 * width)
    return fft_stage3(bv, f3, uv, gv, skip, width=width)


def filter_features(n):
    t = jnp.linspace(0.0, 1.0, n, dtype=F32)[:, None]
    w = (2.0 * math.pi / n) * jnp.arange(n, dtype=F32)[:, None]
    f = jnp.linspace(1e-4, HY_BANDS - 1, HY_BANDS, dtype=F32)[None, :]
    z = jnp.concatenate([t, jnp.cos(f * w), -jnp.sin(f * w)], axis=-1)
    z2 = jnp.concatenate([z, z[:1], z[:0:-1]], axis=0)
    return jnp.pad(z2, ((0, 0), (0, LANES - z2.shape[1])))


def _filter_mlp_kernel(z_ref, w1_ref, b1_ref, f1_ref, w2_ref, b2_ref, f2_ref, o_ref):
    a = jnp.sin(f1_ref[...] * (_dot(z_ref[...].astype(BF16), w1_ref[...].astype(BF16)) + b1_ref[...]))
    o_ref[...] = jnp.sin(f2_ref[...] * (_dot(a.astype(BF16), w2_ref[...].astype(BF16)) + b2_ref[...]))


def filter_mlp(z2, w1, b1, f1, w2, b2, f2, *, tr):
    rows = z2.shape[0]
    hid = w2.shape[0]
    w1p = jnp.pad(w1, ((0, LANES - w1.shape[0]), (0, 0)))
    vec = lambda v: v.reshape(1, hid)
    const = lambda i: (0, 0)
    return pl.pallas_call(
        _filter_mlp_kernel,
        grid=(rows // tr,),
        in_specs=[pl.BlockSpec((tr, LANES), lambda i: (i, 0)), pl.BlockSpec((LANES, hid), const),
                  pl.BlockSpec((1, hid), const), pl.BlockSpec((1, hid), const), pl.BlockSpec((hid, hid), const),
                  pl.BlockSpec((1, hid), const), pl.BlockSpec((1, hid), const)],
        out_specs=pl.BlockSpec((tr, hid), lambda i: (i, 0)),
        out_shape=jax.ShapeDtypeStruct((rows, hid), F32),
        name="filter_mlp",
    )(z2, w1p, vec(b1), vec(f1), w2, vec(b2), vec(f2))


def _filter_kernel(a_ref, t_ref, wf_ref, wb_ref, df_ref, db_ref, o_ref):
    n = a_ref.shape[0] // 2
    a = a_ref[...].astype(BF16)
    t = t_ref[...]
    hf = _dot(a[:n], wf_ref[...].astype(BF16)) * jnp.exp(-t[:n] * jnp.abs(df_ref[...]))
    hb = _dot(a[n:], wb_ref[...].astype(BF16)) * jnp.exp(-t[n:] * jnp.abs(db_ref[...]))
    row = lax.broadcasted_iota(jnp.int32, hb.shape, 0)
    hb0 = hb[0:1]
    hb = jnp.where(row == 0, 0.0, hb)
    norm = (jnp.sum(jnp.abs(hf), axis=0, keepdims=True) + jnp.sum(jnp.abs(hb), axis=0, keepdims=True)
            + jnp.abs(hb0) + 1e-6)
    hf = jnp.where(row == 0, hf + hb0, hf)
    o_ref[0:n, :] = hf / norm
    o_ref[n:, :] = hb / norm


def hyena_kernels(a2, z2, w3, decay, *, width, tc):
    rows, hid = a2.shape
    w3r = w3.reshape(hid, HY_ORDER, 2, width).transpose(1, 2, 0, 3)
    dec = decay.reshape(HY_ORDER, 2, 1, width)
    tcol = z2[:, 0:1]
    wspec = lambda s: pl.BlockSpec((None, None, hid, tc), lambda o, c: (o, s, 0, c))
    dspec = lambda s: pl.BlockSpec((None, None, 1, tc), lambda o, c: (o, s, 0, c))
    return pl.pallas_call(
        _filter_kernel,
        grid=(HY_ORDER, width // tc),
        in_specs=[pl.BlockSpec((rows, hid), lambda o, c: (0, 0)), pl.BlockSpec((rows, 1), lambda o, c: (0, 0)),
                  wspec(0), wspec(1), dspec(0), dspec(1)],
        out_specs=pl.BlockSpec((None, rows, tc), lambda o, c: (o, 0, c)),
        out_shape=jax.ShapeDtypeStruct((HY_ORDER, rows, width), F32),
        compiler_params=_SEQ2,
        name="hyena_filters",
    )(a2, tcol, w3r, w3r, dec, dec)


def filter_spectra(p, consts, *, n, width):
    f1, _, m_fwd, _ = consts
    z2 = filter_features(n)
    a2 = filter_mlp(z2, p['w1'], p['b1'], p['f1'], p['w2'], p['b2'], p['f2'], tr=1024)
    kc = hyena_kernels(a2, z2, p['w3'], p['decay'], width=width, tc=min(HY_FILTER_TILE, width))
    kv = kc.reshape(HY_ORDER, FFT_N2, FFT_N1 * width)
    a = fft_stage1(kv, f1, width=width)
    return fft_filter_mid(a.reshape(HY_ORDER, 2, FFT_K2_PAD, FFT_N1, width), m_fwd)


def _hy_in_kernel(x_ref, xp_ref, xn_ref, sh_ref, sc_ref, g_ref, w_ref, cw_ref, cb_ref, v_ref, x1_ref, x2_ref):
    j = pl.program_id(1)
    nt = pl.num_programs(1)
    tm = x_ref.shape[0]
    width = v_ref.shape[-1]
    mod = lambda x: (_rms(x, g_ref[...]) * (1.0 + sc_ref[...]) + sh_ref[...]).astype(BF16)
    h = mod(x_ref[...])
    h_prev = mod(xp_ref[...])
    h_next = mod(xn_ref[...])
    keep_prev = jnp.where(j > 0, 1.0, 0.0)
    keep_next = jnp.where(j < nt - 1, 1.0, 0.0)
    row = lax.broadcasted_iota(jnp.int32, (tm, width), 0)
    cw = cw_ref[...]
    for part, o_ref in enumerate((v_ref, x1_ref, x2_ref)):
        cols = slice(part * width, (part + 1) * width)
        w = w_ref[:, cols]
        p = _dot(h, w)
        p_prev = _dot(h_prev, w)[SUBLANES - 1:SUBLANES] * keep_prev
        p_next = _dot(h_next, w)[0:1] * keep_next
        before = jnp.where(row == 0, p_prev, pltpu.roll(p, 1, 0))
        after = jnp.where(row == tm - 1, p_next, pltpu.roll(p, tm - 1, 0))
        o_ref[...] = cw[0:1, cols] * before + cw[1:2, cols] * p + cw[2:3, cols] * after + cb_ref[:, cols]


def hy_in_proj(x, shift, scale, norm_g, w_in, conv_w, conv_b, *, n, tm):
    nb, _, d = x.shape
    width = w_in.shape[1] // 3
    rb = tm // SUBLANES
    last = n // SUBLANES - 1
    tok = lambda b, j: (b, j, 0)
    mod_map = lambda b, j: (b, 0, 0)
    const = lambda b, j: (0, 0)
    out = pl.BlockSpec((None, tm, width), tok)
    return pl.pallas_call(
        _hy_in_kernel,
        grid=(nb, n // tm),
        in_specs=[pl.BlockSpec((None, tm, d), tok),
                  pl.BlockSpec((None, SUBLANES, d), lambda b, j: (b, jnp.maximum(j * rb - 1, 0), 0)),
                  pl.BlockSpec((None, SUBLANES, d), lambda b, j: (b, jnp.minimum((j + 1) * rb, last), 0)),
                  pl.BlockSpec((None, 1, d), mod_map), pl.BlockSpec((None, 1, d), mod_map),
                  pl.BlockSpec((1, d), const), pl.BlockSpec(w_in.shape, const),
                  pl.BlockSpec(conv_w.shape, const), pl.BlockSpec((1, 3 * width), const)],
        out_specs=[out, out, out],
        out_shape=[jax.ShapeDtypeStruct((nb, n, width), F32)] * 3,
        compiler_params=_SEQ2,
        name="hy_in_proj",
    )(x, x, x, shift, scale, norm_g.reshape(1, d), w_in.astype(BF16), conv_w, conv_b.reshape(1, -1))


def _hy_out_kernel(x_ref, gate_ref, z_ref, w_ref, o_ref):
    o_ref[...] = x_ref[...] + gate_ref[...] * _dot(z_ref[...].astype(BF16), w_ref[...])


def hy_out_proj(x, gate, z, w_out, *, tm):
    nb, _, d = x.shape
    n = z.shape[1]
    tok = lambda b, j: (b, j, 0)
    return pl.pallas_call(
        _hy_out_kernel,
        grid=(nb, n // tm),
        in_specs=[pl.BlockSpec((None, tm, d), tok), pl.BlockSpec((None, 1, d), lambda b, j: (b, 0, 0)),
                  pl.BlockSpec((None, tm, z.shape[-1]), tok), pl.BlockSpec(w_out.shape, lambda b, j: (0, 0))],
        out_specs=pl.BlockSpec((None, tm, d), tok),
        out_shape=jax.ShapeDtypeStruct((nb, n, d), F32),
        compiler_params=_SEQ2,
        name="hy_out_proj",
    )(x, gate, z, w_out.astype(BF16))


def hyena_mixer(x, shift, scale, gate, norm_g, p, *, n):
    nb = x.shape[0]
    width = p['w_in'].shape[1] // 3
    assert 2 * n == FFT_N
    consts = dft_constants()
    h = filter_spectra(p, consts, n=n, width=width)
    v, x1, x2 = hy_in_proj(x, shift, scale, norm_g, p['w_in'], p['conv_w'], p['conv_b'], n=n, tm=HY_IN_TILE)
    view = lambda a: a.reshape(nb, FFT_N2 // 2, FFT_N1 * width)
    z = long_conv(view(v), view(x1), h[0], p['skip'][0], consts, width=width)
    z = long_conv(z, view(x2), h[1], p['skip'][1], consts, width=width)
    return hy_out_proj(x, gate, z.reshape(nb, n, width), p['w_out'], tm=HY_IN_TILE)


def _final_norm_kernel(x_ref, g_ref, o_ref):
    o_ref[...] = _rms(x_ref[...], g_ref[...])


def final_norm(x, g):
    b, n, d = x.shape
    tm = 1024
    out = pl.pallas_call(
        _final_norm_kernel,
        grid=(b * n // tm,),
        in_specs=[pl.BlockSpec((tm, d), lambda i: (i, 0)), pl.BlockSpec((1, d), lambda i: (0, 0))],
        out_specs=pl.BlockSpec((tm, d), lambda i: (i, 0)),
        out_shape=jax.ShapeDtypeStruct((b * n, d), x.dtype),
        name="final_norm",
    )(x.reshape(b * n, d), g.reshape(1, d))
    return out.reshape(b, n, d)


def kernel(x, c, ctx, c_ctx, ada_w, ada_b, norm1_g, norm2_g, final_g, ev_w_in, mla_q_norm_g, mla_kv_norm_g, mla_w_uq, mla_w_ukv, lru_conv_w, lru_conv_b, lru_w_a, lru_b_a, lru_w_x, lru_b_x, lru_lambda, ev_w_out, od_w_in, hy_conv_w, hy_conv_b, hy_w1, hy_b1, hy_freq1, hy_w2, hy_b2, hy_freq2, hy_w3, hy_decay, hy_skip, od_w_out, router_w, router_b, moe_w1, moe_w3, moe_w2):
    nb, n_lat, d = x.shape
    n_ctx = ctx.shape[1]
    n_ctx_tiles = n_ctx // TOKEN_TILE
    cond = jnp.concatenate([jax.nn.silu(c), jax.nn.silu(c_ctx)[None, :]], axis=0)
    cond = jnp.pad(cond, ((0, 2 * SUBLANES - nb - 1), (0, 0)))

    def mod_rows(layer):
        mod = rows_matmul(cond, ada_w[layer], ada_b[layer], tn=D_MODEL)[:nb + 1]
        return [m[:, None, :] for m in jnp.split(mod, 6, axis=-1)]

    sh1, sc1, g1, sh2, sc2, g2 = mod_rows(0)
    x_all = jnp.concatenate([x, ctx], axis=1)
    p0 = dict(w_in=ev_w_in[0], q_norm_g=mla_q_norm_g[0], kv_norm_g=mla_kv_norm_g[0], w_uq=mla_w_uq[0],
              w_ukv=mla_w_ukv[0], conv_w=lru_conv_w[0], conv_b=lru_conv_b[0], w_a=lru_w_a[0], b_a=lru_b_a[0],
              w_x=lru_w_x[0], b_x=lru_b_x[0], lam=lru_lambda[0], w_out=ev_w_out[0])
    x_all = even_mixer(x_all, sh1, sc1, g1, norm1_g[0], p0, n_lat=n_lat, n_ctx=n_ctx)
    x_all = sparse_moe(x_all, sh2, sc2, g2, norm2_g[0], router_w, router_b, moe_w1[0], moe_w3[0], moe_w2[0],
                       n_ctx_tiles=n_ctx_tiles)

    sh1, sc1, g1, sh2, sc2, g2 = [m[:nb] for m in mod_rows(1)]
    p1 = dict(w_in=od_w_in[0], conv_w=hy_conv_w[0], conv_b=hy_conv_b[0], w1=hy_w1[0], b1=hy_b1[0], f1=hy_freq1[0],
              w2=hy_w2[0], b2=hy_b2[0], f2=hy_freq2[0], w3=hy_w3[0], decay=hy_decay[0], skip=hy_skip[0],
              w_out=od_w_out[0])
    x = hyena_mixer(x_all, sh1, sc1, g1, norm1_g[1], p1, n=n_lat)
    x = sparse_moe(x, sh2, sc2, g2, norm2_g[1], router_w, router_b, moe_w1[1], moe_w3[1], moe_w2[1], n_ctx_tiles=0)
    return final_norm(x, final_g)
```

```python
import functools
import math

import jax
import jax.numpy as jnp
import numpy as np
from jax import lax
from jax.experimental import pallas as pl
from jax.experimental.pallas import tpu as pltpu

D_MODEL = 1024
DEPTH = 2
GRID_W = 64
RMS_EPS = 1e-6

MLA_HEADS = 8
MLA_Q_LORA = 384
MLA_KV_LORA = 256
MLA_NOPE = 64
MLA_ROPE = 32
MLA_V = 64
MLA_QK = MLA_NOPE + MLA_ROPE
ROPE_PAIRS = MLA_ROPE // 4
ROPE_BASE = 10000.0

LRU_WIDTH = 512
LRU_BLOCKS = 8
LRU_BLOCK_DIM = LRU_WIDTH // LRU_BLOCKS
LRU_C = 8.0
LRU_CONV = 4

HY_WIDTH = D_MODEL
HY_ORDER = 2
HY_BANDS = 16
HY_HIDDEN = 64

N_EXPERTS = 16
N_GROUPS = 4
EXPERTS_PER_GROUP = N_EXPERTS // N_GROUPS

LANES = 128
SUBLANES = 8
HEAD_PAD = LANES

TOKEN_TILE = 256
MOE_ROW_TILE = 512
ATTN_Q_TILE = 512
ATTN_KV_CHUNK = 1024
LRU_CHUNK = 256
LRU_SCAN_UNROLL = 8
DMA_LOOP_UNROLL = 8

BF16 = jnp.bfloat16
F32 = jnp.float32
_HI = lax.Precision.HIGHEST
_NEG_INF = float('-inf')


def _dot(a, b):
    return jnp.dot(a, b, preferred_element_type=F32)


def _rms(x, g):
    return x * lax.rsqrt(jnp.mean(x * x, axis=-1, keepdims=True) + RMS_EPS) * g


def _mod_map(nb, nt, n_ctx_tiles):
    if n_ctx_tiles:
        return lambda b, j: (jnp.where(j >= nt - n_ctx_tiles, nb, b), 0, 0)
    return lambda b, j: (b, 0, 0)


_SEQ2 = pltpu.CompilerParams(dimension_semantics=("arbitrary", "arbitrary"))


def _rows_matmul_kernel(x_ref, w_ref, b_ref, o_ref):
    o_ref[...] = _dot(x_ref[...].astype(BF16), w_ref[...].astype(BF16)) + b_ref[...]


def rows_matmul(x, w, b, *, tn):
    m, k = x.shape
    n = w.shape[1]
    return pl.pallas_call(
        _rows_matmul_kernel,
        grid=(n // tn,),
        in_specs=[pl.BlockSpec((m, k), lambda j: (0, 0)),
                  pl.BlockSpec((k, tn), lambda j: (0, j)),
                  pl.BlockSpec((1, tn), lambda j: (0, j))],
        out_specs=pl.BlockSpec((m, tn), lambda j: (0, j)),
        out_shape=jax.ShapeDtypeStruct((m, n), F32),
        name="rows_matmul",
    )(x, w, b.reshape(1, n))


def _rot_cols(w):
    p = ROPE_PAIRS
    return jnp.concatenate([-w[:, p:2 * p], w[:, 0:p], -w[:, 3 * p:4 * p], w[:, 2 * p:3 * p]], axis=1)


def _head_pad_cols(w, width):
    k = w.shape[0]
    w = w.reshape(k, MLA_HEADS, width)
    return jnp.pad(w, ((0, 0), (0, 0), (0, HEAD_PAD - width))).reshape(k, MLA_HEADS * HEAD_PAD)


def prep_even_weights(w_in, w_uq, w_ukv, w_out):
    d = w_in.shape[0]
    s1 = MLA_Q_LORA + MLA_KV_LORA
    s2 = s1 + MLA_ROPE
    w_kr = w_in[:, s1:s2]
    place = lambda w: jnp.pad(w, ((0, 0), (MLA_NOPE, HEAD_PAD - MLA_QK)))
    w_big = jnp.concatenate([w_in[:, :s1], w_in[:, s2:], place(w_kr), place(_rot_cols(w_kr))], axis=1).astype(BF16)
    uq = w_uq.reshape(MLA_Q_LORA, MLA_HEADS, MLA_QK)
    wq_a = _head_pad_cols(w_uq, MLA_QK)
    uq_rot = jnp.stack([_rot_cols(uq[:, h, MLA_NOPE:]) for h in range(MLA_HEADS)], axis=1)
    wq_b = jnp.pad(uq_rot, ((0, 0), (0, 0), (MLA_NOPE, HEAD_PAD - MLA_QK))).reshape(MLA_Q_LORA, MLA_HEADS * HEAD_PAD)
    ukv = w_ukv.reshape(MLA_KV_LORA, MLA_HEADS, MLA_NOPE + MLA_V)
    wk = _head_pad_cols(ukv[:, :, :MLA_NOPE].reshape(MLA_KV_LORA, -1), MLA_NOPE)
    wv = _head_pad_cols(ukv[:, :, MLA_NOPE:].reshape(MLA_KV_LORA, -1), MLA_V)
    w_kv = jnp.concatenate([wk, wv], axis=1)
    att_rows = MLA_HEADS * MLA_V
    wo_att = jnp.pad(w_out[:att_rows].reshape(MLA_HEADS, MLA_V, d),
                     ((0, 0), (0, HEAD_PAD - MLA_V), (0, 0))).reshape(MLA_HEADS * HEAD_PAD, d)
    return dict(w_big=w_big, wq_a=wq_a.astype(BF16), wq_b=wq_b.astype(BF16), w_kv=w_kv.astype(BF16),
                wo_att=wo_att.astype(BF16), wo_lru=w_out[att_rows:].astype(BF16))


def rope_tables(n_lat, n_ctx):
    rows = n_lat // GRID_W
    row = jnp.repeat(jnp.arange(rows), GRID_W)
    col = jnp.tile(jnp.arange(GRID_W), rows)
    inv_freq = ROPE_BASE ** (-jnp.arange(ROPE_PAIRS, dtype=F32) / ROPE_PAIRS)
    ang = jnp.stack([row, col], axis=-1).astype(F32)[:, :, None] * inv_freq
    cos, sin = jnp.cos(ang), jnp.sin(ang)
    c32 = jnp.concatenate([cos[:, 0], cos[:, 0], cos[:, 1], cos[:, 1]], axis=-1)
    s32 = jnp.concatenate([sin[:, 0], sin[:, 0], sin[:, 1], sin[:, 1]], axis=-1)
    c32 = jnp.concatenate([c32, jnp.ones((n_ctx, MLA_ROPE), F32)], axis=0)
    s32 = jnp.concatenate([s32, jnp.zeros((n_ctx, MLA_ROPE), F32)], axis=0)
    n = n_lat + n_ctx
    pad_hi = jnp.zeros((n, HEAD_PAD - MLA_QK), F32)
    scale = MLA_QK ** -0.5
    cq = jnp.concatenate([jnp.full((n, MLA_NOPE), scale, F32), c32 * scale, pad_hi], axis=-1)
    sq = jnp.concatenate([jnp.zeros((n, MLA_NOPE), F32), s32 * scale, pad_hi], axis=-1)
    ck = jnp.concatenate([jnp.zeros((n, MLA_NOPE), F32), c32, pad_hi], axis=-1)
    sk = jnp.concatenate([jnp.zeros((n, MLA_NOPE), F32), s32, pad_hi], axis=-1)
    return cq, sq, ck, sk


def _even_in_kernel(x_ref, sh_ref, sc_ref, g_ref, wbig_ref, qg_ref, kvg_ref, wqa_ref, wqb_ref, wkv_ref,
                    cq_ref, sq_ref, ck_ref, sk_ref, q_ref, k_ref, v_ref, ux_ref, ug_ref):
    h = (_rms(x_ref[...], g_ref[...]) * (1.0 + sc_ref[...]) + sh_ref[...]).astype(BF16)
    p = _dot(h, wbig_ref[...])
    o1 = MLA_Q_LORA
    o2 = o1 + MLA_KV_LORA
    o3 = o2 + LRU_WIDTH
    o4 = o3 + LRU_WIDTH
    o5 = o4 + HEAD_PAD
    ux_ref[...] = p[:, o2:o3]
    ug_ref[...] = p[:, o3:o4]
    cqn = _rms(p[:, :o1], qg_ref[...]).astype(BF16)
    rep = lambda t: jnp.concatenate([t] * MLA_HEADS, axis=-1)
    q = _dot(cqn, wqa_ref[...]) * rep(cq_ref[...]) + _dot(cqn, wqb_ref[...]) * rep(sq_ref[...])
    q_ref[...] = q.astype(BF16)
    ckvn = _rms(p[:, o1:o2], kvg_ref[...]).astype(BF16)
    kv = _dot(ckvn, wkv_ref[...])
    k_rope = p[:, o4:o5] * ck_ref[...] + p[:, o5:] * sk_ref[...]
    hw = MLA_HEADS * HEAD_PAD
    k_ref[...] = (kv[:, :hw] + rep(k_rope)).astype(BF16)
    lane = lax.broadcasted_iota(jnp.int32, (1, hw), 1)
    ones_col = jnp.where((lane & (HEAD_PAD - 1)) == MLA_V, 1.0, 0.0)
    v_ref[...] = (kv[:, hw:] + ones_col).astype(BF16)


def even_in_proj(x, shift, scale, norm_g, wts, q_norm_g, kv_norm_g, tables, *, tm, n_ctx_tiles):
    nb, s_len, d = x.shape
    nt = s_len // tm
    hw = MLA_HEADS * HEAD_PAD
    mod_map = _mod_map(nb, nt, n_ctx_tiles)
    const = lambda b, j: (0, 0)
    tok = lambda b, j: (b, j, 0)
    tab = lambda b, j: (j, 0)
    full = lambda a: pl.BlockSpec(a.shape, const)
    row = lambda v: v.reshape(1, -1)
    args = [x, shift, scale, row(norm_g), wts['w_big'], row(q_norm_g), row(kv_norm_g),
            wts['wq_a'], wts['wq_b'], wts['w_kv']]
    in_specs = [pl.BlockSpec((None, tm, d), tok), pl.BlockSpec((None, 1, d), mod_map),
                pl.BlockSpec((None, 1, d), mod_map)]
    in_specs += [full(a) for a in args[3:]]
    in_specs += [pl.BlockSpec((tm, HEAD_PAD), tab)] * 4
    out_dims = [(hw, BF16), (hw, BF16), (hw, BF16), (LRU_WIDTH, F32), (LRU_WIDTH, F32)]
    return pl.pallas_call(
        _even_in_kernel,
        grid=(nb, nt),
        in_specs=in_specs,
        out_specs=[pl.BlockSpec((None, tm, w), tok) for w, _ in out_dims],
        out_shape=[jax.ShapeDtypeStruct((nb, s_len, w), dt) for w, dt in out_dims],
        compiler_params=_SEQ2,
        name="even_in_proj",
    )(*args, *tables)


def _attn_kernel(q_ref, k_ref, v_ref, o_ref):
    q = q_ref[...]
    tq = q.shape[0]
    n_k = k_ref.shape[0]
    m = jnp.full((tq, 1), _NEG_INF, F32)
    acc = jnp.zeros((tq, HEAD_PAD), F32)
    for start in range(0, n_k, ATTN_KV_CHUNK):
        size = min(ATTN_KV_CHUNK, n_k - start)
        s = lax.dot_general(q, k_ref[start:start + size, :], (((1,), (1,)), ((), ())), preferred_element_type=F32)
        m_new = jnp.maximum(m, jnp.max(s, axis=-1, keepdims=True))
        p = jnp.exp(s - m_new).astype(BF16)
        acc = jnp.exp(m - m_new) * acc + _dot(p, v_ref[start:start + size, :])
        m = m_new
    o_ref[...] = (acc / acc[:, MLA_V:MLA_V + 1]).astype(o_ref.dtype)


def attention(q, k, v, *, q_start, n_q, k_start, n_k, tq):
    nb = q.shape[0]
    qo = q_start // tq
    ko = k_start // n_k
    return pl.pallas_call(
        _attn_kernel,
        grid=(nb, MLA_HEADS, n_q // tq),
        in_specs=[pl.BlockSpec((None, tq, HEAD_PAD), lambda b, h, i: (b, qo + i, h)),
                  pl.BlockSpec((None, n_k, HEAD_PAD), lambda b, h, i: (b, ko, h)),
                  pl.BlockSpec((None, n_k, HEAD_PAD), lambda b, h, i: (b, ko, h))],
        out_specs=pl.BlockSpec((None, tq, HEAD_PAD), lambda b, h, i: (b, i, h)),
        out_shape=jax.ShapeDtypeStruct((nb, n_q, MLA_HEADS * HEAD_PAD), BF16),
        compiler_params=pltpu.CompilerParams(dimension_semantics=("arbitrary", "arbitrary", "arbitrary")),
        name="attention",
    )(q, k, v)


def _scan_block(a, b, h_prev, row, reverse):
    for s in (1, 2, 4):
        sh = SUBLANES - s if reverse else s
        a_s = pltpu.roll(a, sh, 0)
        b_s = pltpu.roll(b, sh, 0)
        ok = (row < SUBLANES - s) if reverse else (row >= s)
        b = jnp.where(ok, a * b_s + b, b)
        a = jnp.where(ok, a * a_s, a)
    return a * h_prev + b


def _rglru_kernel(ux_ref, ug_ref, cw_ref, cb_ref, wa_ref, ba_ref, wx_ref, bx_ref, lam_ref, o_ref,
                  xp_ref, a_ref, b_ref, hf_ref, hb_ref, *, n_lat, n_ctx):
    lanes = ux_ref.shape[-1]
    pad = SUBLANES
    zeros = jnp.zeros((pad, lanes), F32)
    lat0 = pad
    ctx0 = 2 * pad + n_lat
    xp_ref[0:pad, :] = zeros
    xp_ref[lat0 + n_lat:ctx0, :] = zeros
    xp_ref[ctx0 + n_ctx:ctx0 + n_ctx + pad, :] = zeros
    xp_ref[lat0:lat0 + n_lat, :] = ux_ref[0:n_lat, :]
    xp_ref[ctx0:ctx0 + n_ctx, :] = ux_ref[n_lat:n_lat + n_ctx, :]
    cw = cw_ref[...]
    n_chunks = (n_lat + n_ctx) // LRU_CHUNK
    n_lat_chunks = n_lat // LRU_CHUNK
    n_win = LRU_CHUNK + 2 * pad

    def coeffs(c, carry):
        src = pl.multiple_of(c * LRU_CHUNK + jnp.where(c >= n_lat_chunks, pad, 0), SUBLANES)
        dst = pl.multiple_of(c * LRU_CHUNK, SUBLANES)
        win = xp_ref[pl.ds(src, n_win), :]
        mid = lambda t: t[pad:pad + LRU_CHUNK]
        u = cb_ref[...] + cw[2:3, :] * mid(win)
        u = u + cw[0:1, :] * mid(pltpu.roll(win, 2, 0))
        u = u + cw[1:2, :] * mid(pltpu.roll(win, 1, 0))
        u = u + cw[3:4, :] * mid(pltpu.roll(win, n_win - 1, 0))
        ub = u.astype(BF16)
        for d in range(2):
            r = jax.nn.sigmoid(_dot(ub, wa_ref[d]) + ba_ref[d])
            i = jax.nn.sigmoid(_dot(ub, wx_ref[d]) + bx_ref[d])
            a = jnp.exp(lam_ref[d] * r)
            a_ref[d, pl.ds(dst, LRU_CHUNK), :] = a
            b_ref[d, pl.ds(dst, LRU_CHUNK), :] = jnp.sqrt(1.0 - a * a) * (i * u)
        return carry

    lax.fori_loop(0, n_chunks, coeffs, 0)
    row = lax.broadcasted_iota(jnp.int32, (SUBLANES, lanes), 0)

    def make_step(first_blk, n_blk):
        def step(t, carry):
            h_f, h_b = carry
            rf = pl.multiple_of((first_blk + t) * SUBLANES, SUBLANES)
            rb = pl.multiple_of((first_blk + n_blk - 1 - t) * SUBLANES, SUBLANES)
            out_f = _scan_block(a_ref[0, pl.ds(rf, SUBLANES), :], b_ref[0, pl.ds(rf, SUBLANES), :], h_f, row, False)
            out_b = _scan_block(a_ref[1, pl.ds(rb, SUBLANES), :], b_ref[1, pl.ds(rb, SUBLANES), :], h_b, row, True)
            hf_ref[pl.ds(rf, SUBLANES), :] = out_f
            hb_ref[pl.ds(rb, SUBLANES), :] = out_b
            h_f = jnp.broadcast_to(out_f[SUBLANES - 1:SUBLANES, :], out_f.shape)
            h_b = jnp.broadcast_to(out_b[0:1, :], out_b.shape)
            return h_f, h_b
        return step

    state = (jnp.zeros((SUBLANES, lanes), F32), jnp.zeros((SUBLANES, lanes), F32))
    state = lax.fori_loop(0, n_ctx // SUBLANES, make_step(n_lat // SUBLANES, n_ctx // SUBLANES), state,
                          unroll=LRU_SCAN_UNROLL)
    lax.fori_loop(0, n_lat // SUBLANES, make_step(0, n_lat // SUBLANES), state, unroll=LRU_SCAN_UNROLL)

    def gate(c, carry):
        r0 = pl.multiple_of(c * LRU_CHUNK, SUBLANES)
        g = ug_ref[pl.ds(r0, LRU_CHUNK), :]
        gelu = 0.5 * g * (1.0 + jnp.tanh(math.sqrt(2.0 / math.pi) * (g + 0.044715 * (g * g * g))))
        y = hf_ref[pl.ds(r0, LRU_CHUNK), :] + hb_ref[pl.ds(r0, LRU_CHUNK), :]
        o_ref[pl.ds(r0, LRU_CHUNK), :] = (y * gelu).astype(o_ref.dtype)
        return carry

    lax.fori_loop(0, n_chunks, gate, 0)


def _block_diag(w):
    per = LANES // LRU_BLOCK_DIM
    nt = LRU_BLOCKS // per
    w = w.reshape(2, nt, per, LRU_BLOCK_DIM, LRU_BLOCK_DIM)
    eye = jnp.eye(per, dtype=w.dtype)
    out = jnp.einsum('dtpij,pq->dtpiqj', w, eye)
    return out.reshape(2, nt, LANES, LANES)


def rglru(ux, ug, conv_w, conv_b, w_a, b_a, w_x, b_x, lam, *, n_lat, n_ctx):
    nb, s_len, width = ux.shape
    nt = width // LANES
    wa = _block_diag(w_a).astype(BF16).transpose(1, 0, 2, 3)
    wx = _block_diag(w_x).astype(BF16).transpose(1, 0, 2, 3)
    lam_c = -LRU_C * jax.nn.softplus(-lam.astype(F32))
    lane3 = lambda v: v.reshape(2, 1, width)
    seq = pl.BlockSpec((None, s_len, LANES), lambda b, c: (b, 0, c))
    vec = lambda rows: pl.BlockSpec((rows, LANES), lambda b, c: (0, c))
    dvec = pl.BlockSpec((2, 1, LANES), lambda b, c: (0, 0, c))
    wspec = pl.BlockSpec((None, 2, LANES, LANES), lambda b, c: (c, 0, 0, 0))
    return pl.pallas_call(
        functools.partial(_rglru_kernel, n_lat=n_lat, n_ctx=n_ctx),
        grid=(nb, nt),
        in_specs=[seq, seq, vec(LRU_CONV), vec(1), wspec, dvec, wspec, dvec, dvec],
        out_specs=seq,
        out_shape=jax.ShapeDtypeStruct((nb, s_len, width), BF16),
        scratch_shapes=[pltpu.VMEM((s_len + 3 * SUBLANES, LANES), F32),
                        pltpu.VMEM((2, s_len, LANES), F32), pltpu.VMEM((2, s_len, LANES), F32),
                        pltpu.VMEM((s_len, LANES), F32), pltpu.VMEM((s_len, LANES), F32)],
        compiler_params=_SEQ2,
        name="rglru",
    )(ux, ug, conv_w, conv_b.reshape(1, width), wa, lane3(b_a), wx, lane3(b_x), lane3(lam_c))


def _even_out_kernel(x_ref, gate_ref, att_ref, lru_ref, wa_ref, wl_ref, o_ref):
    y = _dot(att_ref[...], wa_ref[...]) + _dot(lru_ref[...], wl_ref[...])
    o_ref[...] = x_ref[...] + gate_ref[...] * y


def even_out_proj(x, gate, att, lru, wo_att, wo_lru, *, tm, n_ctx_tiles):
    nb, s_len, d = x.shape
    nt = s_len // tm
    tok = lambda b, j: (b, j, 0)
    const = lambda b, j: (0, 0)
    return pl.pallas_call(
        _even_out_kernel,
        grid=(nb, nt),
        in_specs=[pl.BlockSpec((None, tm, d), tok), pl.BlockSpec((None, 1, d), _mod_map(nb, nt, n_ctx_tiles)),
                  pl.BlockSpec((None, tm, att.shape[-1]), tok), pl.BlockSpec((None, tm, lru.shape[-1]), tok),
                  pl.BlockSpec(wo_att.shape, const), pl.BlockSpec(wo_lru.shape, const)],
        out_specs=pl.BlockSpec((None, tm, d), tok),
        out_shape=jax.ShapeDtypeStruct((nb, s_len, d), F32),
        compiler_params=_SEQ2,
        name="even_out_proj",
    )(x, gate, att, lru, wo_att, wo_lru)


def even_mixer(x_all, shift, scale, gate, norm_g, p, *, n_lat, n_ctx):
    n_ctx_tiles = n_ctx // TOKEN_TILE
    tables = rope_tables(n_lat, n_ctx)
    wts = prep_even_weights(p['w_in'], p['w_uq'], p['w_ukv'], p['w_out'])
    q, k, v, ux, ug = even_in_proj(x_all, shift, scale, norm_g, wts, p['q_norm_g'], p['kv_norm_g'], tables,
                                   tm=TOKEN_TILE, n_ctx_tiles=n_ctx_tiles)
    att_l = attention(q, k, v, q_start=0, n_q=n_lat, k_start=0, n_k=n_lat + n_ctx, tq=ATTN_Q_TILE)
    att_c = attention(q, k, v, q_start=n_lat, n_q=n_ctx, k_start=n_lat, n_k=n_ctx, tq=n_ctx)
    att = jnp.concatenate([att_l, att_c], axis=1)
    lru = rglru(ux, ug, p['conv_w'], p['conv_b'], p['w_a'], p['b_a'], p['w_x'], p['b_x'], p['lam'],
                n_lat=n_lat, n_ctx=n_ctx)
    return even_out_proj(x_all, gate, att, lru, wts['wo_att'], wts['wo_lru'], tm=TOKEN_TILE, n_ctx_tiles=n_ctx_tiles)


def _load_token_tiles(ref, n_tok, d):
    return jnp.concatenate([ref[pl.ds(c, n_tok, stride=SUBLANES), :] for c in range(d // LANES)], axis=1)


def _store_token_tiles(ref, val):
    n_tok, d = val.shape
    for c in range(d // LANES):
        ref[pl.ds(c, n_tok, stride=SUBLANES), :] = val[:, c * LANES:(c + 1) * LANES]


def _first_argmax(vals):
    best = vals[0]
    idx = jnp.zeros(vals[0].shape, jnp.int32)
    for j in range(1, len(vals)):
        better = vals[j] > best
        idx = jnp.where(better, j, idx)
        best = jnp.where(better, vals[j], best)
    return idx, best


def _select(idx, vals):
    out = vals[0]
    for j in range(1, len(vals)):
        out = jnp.where(idx == j, vals[j], out)
    return out


def _moe_route_kernel(x_ref, sh_ref, sc_ref, g_ref, rwt_ref, rb_ref,
                      h_ref, e_ref, w_ref, rank_ref, cnt_ref, carry_ref, *, n_groups, per_group):
    first = jnp.logical_and(pl.program_id(0) == 0, pl.program_id(1) == 0)

    @pl.when(first)
    def _():
        carry_ref[...] = jnp.zeros_like(carry_ref)

    tm = x_ref.shape[0]
    n_exp = n_groups * per_group
    h = _rms(x_ref[...], g_ref[...]) * (1.0 + sc_ref[...]) + sh_ref[...]
    _store_token_tiles(h_ref, h)
    logits = lax.dot_general(rwt_ref[...], h, (((1,), (1,)), ((), ())),
                             precision=_HI, preferred_element_type=F32)
    s = jax.nn.sigmoid(logits)
    sel = s + rb_ref[...]
    sel_rows = [sel[e:e + 1, :] for e in range(n_exp)]
    s_rows = [s[e:e + 1, :] for e in range(n_exp)]
    g_scores = []
    for g in range(n_groups):
        r = sel_rows[g * per_group:(g + 1) * per_group]
        pair_sums = [r[a] + r[b] for a in range(per_group) for b in range(a + 1, per_group)]
        g_scores.append(functools.reduce(jnp.maximum, pair_sums))
    g_idx, _ = _first_argmax(g_scores)
    v = [_select(g_idx, [sel_rows[g * per_group + j] for g in range(n_groups)]) for j in range(per_group)]
    sv = [_select(g_idx, [s_rows[g * per_group + j] for g in range(n_groups)]) for j in range(per_group)]
    i1, _ = _first_argmax(v)
    i2, _ = _first_argmax([jnp.full_like(v[0], _NEG_INF)]
                          + [jnp.where(i1 == j, _NEG_INF, v[j]) for j in range(per_group)])
    i2 = i2 - 1
    w1 = _select(i1, sv)
    w2 = _select(i2, sv)
    wsum = w1 + w2
    e1 = g_idx * per_group + i1
    e2 = g_idx * per_group + i2
    e_ref[0:1, :] = e1
    e_ref[1:2, :] = e2
    w_ref[0:1, :] = w1 / wsum
    w_ref[1:2, :] = w2 / wsum
    e_iota = lax.broadcasted_iota(jnp.int32, (n_exp, tm), 0)
    oh1 = e_iota == e1
    oh2 = e_iota == e2
    m = jnp.where(oh1, 1.0, jnp.where(oh2, 1.0, 0.0))
    r_i = lax.broadcasted_iota(jnp.int32, (tm, tm), 0)
    c_i = lax.broadcasted_iota(jnp.int32, (tm, tm), 1)
    upper = jnp.where(r_i < c_i, 1.0, 0.0).astype(BF16)
    pre = _dot(m.astype(BF16), upper)
    tot = pre + carry_ref[:, 0:1]
    rank_ref[0:1, :] = jnp.sum(jnp.where(oh1, tot, 0.0), axis=0, keepdims=True).astype(jnp.int32)
    rank_ref[1:2, :] = jnp.sum(jnp.where(oh2, tot, 0.0), axis=0, keepdims=True).astype(jnp.int32)
    new_carry = carry_ref[...] + jnp.sum(m, axis=1, keepdims=True)
    carry_ref[...] = new_carry
    cnt_ref[...] = new_carry


def moe_route(x, shift, scale, norm_g, router_w, router_b, *, tm, n_ctx_tiles, n_groups, per_group):
    nb, s_len, d = x.shape
    nt = s_len // tm
    n_exp = n_groups * per_group
    rwt = router_w.T
    rb = router_b.reshape(n_exp, 1)
    mod_map = _mod_map(nb, nt, n_ctx_tiles)
    tok_map = lambda b, j: (b * nt + j, 0, 0)
    return pl.pallas_call(
        functools.partial(_moe_route_kernel, n_groups=n_groups, per_group=per_group),
        grid=(nb, nt),
        in_specs=[
            pl.BlockSpec((None, tm, d), lambda b, j: (b, j, 0)),
            pl.BlockSpec((None, 1, d), mod_map),
            pl.BlockSpec((None, 1, d), mod_map),
            pl.BlockSpec((1, d), lambda b, j: (0, 0)),
            pl.BlockSpec((n_exp, d), lambda b, j: (0, 0)),
            pl.BlockSpec((n_exp, 1), lambda b, j: (0, 0)),
        ],
        out_specs=[
            pl.BlockSpec((tm * SUBLANES, d // SUBLANES), lambda b, j: (b * nt + j, 0)),
            pl.BlockSpec((None, 2, tm), tok_map),
            pl.BlockSpec((None, 2, tm), tok_map),
            pl.BlockSpec((None, 2, tm), tok_map),
            pl.BlockSpec((n_exp, LANES), lambda b, j: (0, 0)),
        ],
        out_shape=[
            jax.ShapeDtypeStruct((nb * s_len * SUBLANES, d // SUBLANES), F32),
            jax.ShapeDtypeStruct((nb * nt, 2, tm), jnp.int32),
            jax.ShapeDtypeStruct((nb * nt, 2, tm), F32),
            jax.ShapeDtypeStruct((nb * nt, 2, tm), jnp.int32),
            jax.ShapeDtypeStruct((n_exp, LANES), F32),
        ],
        scratch_shapes=[pltpu.VMEM((n_exp, LANES), F32)],
        compiler_params=_SEQ2,
        name="moe_route",
    )(x, shift, scale, norm_g.reshape(1, d), rwt, rb)


def _row_copy(src_ref, src_tok, dst_ref, dst_tok, sem):
    src = src_ref.at[pl.ds(pl.multiple_of(src_tok * SUBLANES, SUBLANES), SUBLANES)]
    dst = dst_ref.at[pl.ds(pl.multiple_of(dst_tok * SUBLANES, SUBLANES), SUBLANES)]
    return pltpu.make_async_copy(src, dst, sem)


def _moe_dispatch_kernel(pos_ref, h_ref, xs_in_ref, xs_ref, sem):
    del xs_in_ref
    tm = h_ref.shape[0] // SUBLANES

    def start(r, c):
        _row_copy(h_ref, r, xs_ref, pos_ref[0, r], sem).start()
        _row_copy(h_ref, r, xs_ref, pos_ref[1, r], sem).start(priority=1)
        return c

    lax.fori_loop(0, tm, start, 0, unroll=DMA_LOOP_UNROLL)

    def wait(r, c):
        _row_copy(h_ref, 0, xs_ref, 0, sem).wait()
        _row_copy(h_ref, 0, xs_ref, 0, sem).wait()
        return c

    lax.fori_loop(0, tm, wait, 0, unroll=DMA_LOOP_UNROLL)


def moe_dispatch(h_tiles, pos, n_rows, *, tm):
    rows, lanes = h_tiles.shape
    zeros = jnp.zeros((n_rows * SUBLANES, lanes), h_tiles.dtype)
    return pl.pallas_call(
        _moe_dispatch_kernel,
        grid=(rows // (tm * SUBLANES),),
        in_specs=[
            pl.BlockSpec((None, 2, tm), lambda i: (i, 0, 0), memory_space=pltpu.SMEM),
            pl.BlockSpec((tm * SUBLANES, lanes), lambda i: (i, 0)),
            pl.BlockSpec(memory_space=pl.ANY),
        ],
        out_specs=pl.BlockSpec(memory_space=pl.ANY),
        out_shape=jax.ShapeDtypeStruct(zeros.shape, h_tiles.dtype),
        scratch_shapes=[pltpu.SemaphoreType.DMA(())],
        input_output_aliases={2: 0},
        compiler_params=pltpu.CompilerParams(dimension_semantics=("arbitrary",), has_side_effects=True),
        name="moe_dispatch",
    )(pos, h_tiles, zeros)


def _moe_expert_kernel(te_ref, nv_ref, xs_ref, w1_ref, w3_ref, w2_ref, ys_ref):
    del te_ref

    @pl.when(pl.program_id(0) < nv_ref[0])
    def _():
        d = w1_ref.shape[0]
        x = _load_token_tiles(xs_ref, xs_ref.shape[0] // SUBLANES, d).astype(BF16)
        a = _dot(x, w1_ref[...])
        b = _dot(x, w3_ref[...])
        act = (a * jax.nn.sigmoid(a) * b).astype(BF16)
        _store_token_tiles(ys_ref, _dot(act, w2_ref[...]))

    @pl.when(pl.program_id(0) >= nv_ref[0])
    def _():
        ys_ref[...] = jnp.zeros_like(ys_ref)


def moe_experts(xs, tile_expert, n_valid, w1, w3, w2, *, tmm):
    rows, lanes = xs.shape
    n_tiles = rows // (tmm * SUBLANES)
    d, ff = w1.shape[-2:]
    row_map = lambda i, te, nv: (jnp.minimum(i, nv[0] - 1), 0)
    w_map = lambda i, te, nv: (te[jnp.minimum(i, nv[0] - 1)], 0, 0)
    grid_spec = pltpu.PrefetchScalarGridSpec(
        num_scalar_prefetch=2,
        grid=(n_tiles,),
        in_specs=[
            pl.BlockSpec((tmm * SUBLANES, lanes), row_map),
            pl.BlockSpec((None, d, ff), w_map),
            pl.BlockSpec((None, d, ff), w_map),
            pl.BlockSpec((None, ff, d), w_map),
        ],
        out_specs=pl.BlockSpec((tmm * SUBLANES, lanes), lambda i, te, nv: (i, 0)),
    )
    return pl.pallas_call(
        _moe_expert_kernel,
        grid_spec=grid_spec,
        out_shape=jax.ShapeDtypeStruct(xs.shape, F32),
        compiler_params=pltpu.CompilerParams(dimension_semantics=("arbitrary",)),
        name="moe_experts",
    )(tile_expert, n_valid, xs, w1, w3, w2)


def _moe_combine_kernel(pos_ref, x_ref, gate_ref, w_ref, ys_ref, o_ref, buf0_ref, buf1_ref, sem):
    tm, d = x_ref.shape

    def start(r, c):
        _row_copy(ys_ref, pos_ref[0, r], buf0_ref, r, sem).start()
        _row_copy(ys_ref, pos_ref[1, r], buf1_ref, r, sem).start(priority=1)
        return c

    lax.fori_loop(0, tm, start, 0, unroll=DMA_LOOP_UNROLL)

    def wait(r, c):
        _row_copy(ys_ref, 0, buf0_ref, 0, sem).wait()
        _row_copy(ys_ref, 0, buf1_ref, 0, sem).wait()
        return c

    lax.fori_loop(0, tm, wait, 0, unroll=DMA_LOOP_UNROLL)
    w = w_ref[...]
    y = w[:, 0:1] * _load_token_tiles(buf0_ref, tm, d) + w[:, 1:2] * _load_token_tiles(buf1_ref, tm, d)
    o_ref[...] = x_ref[...] + gate_ref[...] * y


def moe_combine(x, gate, wts, pos, ys, *, tm, n_ctx_tiles):
    nb, s_len, d = x.shape
    nt = s_len // tm
    return pl.pallas_call(
        _moe_combine_kernel,
        grid=(nb, nt),
        in_specs=[
            pl.BlockSpec((None, 2, tm), lambda b, j: (b * nt + j, 0, 0), memory_space=pltpu.SMEM),
            pl.BlockSpec((None, tm, d), lambda b, j: (b, j, 0)),
            pl.BlockSpec((None, 1, d), _mod_map(nb, nt, n_ctx_tiles)),
            pl.BlockSpec((tm, 2), lambda b, j: (b * nt + j, 0)),
            pl.BlockSpec(memory_space=pl.ANY),
        ],
        out_specs=pl.BlockSpec((None, tm, d), lambda b, j: (b, j, 0)),
        out_shape=jax.ShapeDtypeStruct((nb, s_len, d), F32),
        scratch_shapes=[pltpu.VMEM((tm * SUBLANES, d // SUBLANES), F32), pltpu.VMEM((tm * SUBLANES, d // SUBLANES), F32),
                        pltpu.SemaphoreType.DMA(())],
        compiler_params=_SEQ2,
        name="moe_combine",
    )(pos, x, gate, wts, ys)


def sparse_moe(x, shift, scale, gate, norm_g, router_w, router_b, w1, w3, w2, *, n_ctx_tiles):
    nb, s_len, d = x.shape
    assert d == SUBLANES * LANES
    tm, tmm = TOKEN_TILE, MOE_ROW_TILE
    t = nb * s_len
    h, e, w, rank, cnt = moe_route(x, shift, scale, norm_g, router_w, router_b, tm=tm, n_ctx_tiles=n_ctx_tiles,
                                   n_groups=N_GROUPS, per_group=EXPERTS_PER_GROUP)
    counts = cnt[:, 0].astype(jnp.int32)
    padded = ((counts + tmm - 1) // tmm) * tmm
    ends = jnp.cumsum(padded)
    offs = ends - padded
    pos = offs[e] + rank
    n_tiles = (2 * t + N_EXPERTS * (tmm - 1)) // tmm + 1
    n_valid = (ends[-1] // tmm).astype(jnp.int32).reshape(1)
    tile_expert = jnp.minimum(jnp.searchsorted(ends, jnp.arange(n_tiles, dtype=jnp.int32) * tmm, side='right'),
                              N_EXPERTS - 1).astype(jnp.int32)
    xs = moe_dispatch(h, pos, n_tiles * tmm, tm=tm)
    ys = moe_experts(xs, tile_expert, n_valid, w1.astype(BF16), w3.astype(BF16), w2.astype(BF16), tmm=tmm)
    wts = w.transpose(0, 2, 1).reshape(t, 2)
    return moe_combine(x, gate, wts, pos, ys, tm=tm, n_ctx_tiles=n_ctx_tiles)


FFT_N1 = 64
FFT_N2 = 128
FFT_N = FFT_N1 * FFT_N2
FFT_K2 = FFT_N2 // 2 + 1
FFT_K2_PAD = 72
FFT_COLS = 16
FFT_A_PITCH = 152
FFT_B_PITCH = 24
HY_VIEW_PITCH = 72
HY_IN_TILE = 512
HY_FILTER_TILE = 256


def dft_constants():
    n1 = np.arange(FFT_N1)
    n2 = np.arange(FFT_N2)
    k2 = np.arange(FFT_K2_PAD)
    n = n1[:, None, None] + FFT_N1 * n2[None, None, :]
    ang = 2.0 * np.pi * ((k2[None, :, None] * n) % FFT_N) / FFT_N
    live = (k2 < FFT_K2)[None, :, None]
    f1 = np.concatenate([np.cos(ang) * live, -np.sin(ang) * live], axis=1)
    wgt = np.where((k2 == 0) | (k2 == FFT_N2 // 2), 1.0, 2.0) * (k2 < FFT_K2) / FFT_N
    ang3 = np.transpose(ang, (0, 2, 1))
    f3 = np.concatenate([np.cos(ang3) * wgt, -np.sin(ang3) * wgt], axis=2)
    f3 = f3[:, :FFT_N2 // 2]
    a = 2.0 * np.pi * ((n1[:, None] * n1[None, :]) % FFT_N1) / FFT_N1
    cr, ci = np.cos(a), -np.sin(a)
    m_fwd = np.block([[cr, -ci], [ci, cr]])
    m_inv = np.block([[cr, ci], [-ci, cr]])
    return (jnp.asarray(f1, BF16), jnp.asarray(f3, BF16), jnp.asarray(m_fwd, BF16), jnp.asarray(m_inv, BF16))


def _fft_stage1_kernel(x_ref, f_ref, o_ref, scr_ref, *, width):
    slabs = width // LANES
    slab_rows = FFT_COLS * FFT_A_PITCH
    for g in range(FFT_COLS):
        res = _dot(f_ref[g], x_ref[:, g * width:(g + 1) * width].astype(BF16))
        for c in range(slabs):
            scr_ref[pl.ds(c * slab_rows + g * FFT_A_PITCH, 2 * FFT_K2_PAD), :] = res[:, c * LANES:(c + 1) * LANES]

    def row(r, carry):
        for part in range(2):
            for c in range(slabs):
                base = c * slab_rows + part * FFT_K2_PAD + r
                lo = scr_ref[pl.ds(base, SUBLANES, stride=FFT_A_PITCH), :]
                hi = scr_ref[pl.ds(base + SUBLANES * FFT_A_PITCH, SUBLANES, stride=FFT_A_PITCH), :]
                o_ref[part, r, :, c * LANES:(c + 1) * LANES] = jnp.concatenate([lo, hi], axis=0).astype(o_ref.dtype)
        return carry

    lax.fori_loop(0, FFT_K2_PAD, row, 0)


def fft_stage1(xv, f1, *, width):
    nb, k, cols = xv.shape
    blk = FFT_COLS * width
    return pl.pallas_call(
        functools.partial(_fft_stage1_kernel, width=width),
        grid=(nb, FFT_N1 // FFT_COLS),
        in_specs=[pl.BlockSpec((None, k, blk), lambda b, j: (b, 0, j)),
                  pl.BlockSpec((FFT_COLS, 2 * FFT_K2_PAD, k), lambda b, j: (j, 0, 0))],
        out_specs=pl.BlockSpec((None, 2, FFT_K2_PAD, FFT_COLS, width), lambda b, j: (b, 0, 0, j, 0)),
        out_shape=jax.ShapeDtypeStruct((nb, 2, FFT_K2_PAD, FFT_N1, width), BF16),
        scratch_shapes=[pltpu.VMEM((width // LANES * FFT_COLS * FFT_A_PITCH, LANES), F32)],
        compiler_params=_SEQ2,
        name="fft_stage1",
    )(xv, f1[:, :, :k])


def _fft_mid_kernel(a_ref, h_ref, mf_ref, mi_ref, o_ref):
    for kk in range(SUBLANES):
        a = jnp.concatenate([a_ref[0, kk], a_ref[1, kk]], axis=0)
        x = _dot(mf_ref[...], a)
        xr, xi = x[:FFT_N1], x[FFT_N1:]
        hr, hi = h_ref[kk, :FFT_N1], h_ref[kk, FFT_N1:]
        y = jnp.concatenate([xr * hr - xi * hi, xr * hi + xi * hr], axis=0).astype(BF16)
        b = _dot(mi_ref[...], y)
        o_ref[0, kk] = b[:FFT_N1].astype(o_ref.dtype)
        o_ref[1, kk] = b[FFT_N1:].astype(o_ref.dtype)


def fft_mid(a5, h, m_fwd, m_inv):
    nb, _, k2p, n1, c = a5.shape
    blk = pl.BlockSpec((None, 2, SUBLANES, n1, c), lambda j, b: (b, 0, j, 0, 0))
    return pl.pallas_call(
        _fft_mid_kernel,
        grid=(k2p // SUBLANES, nb),
        in_specs=[blk, pl.BlockSpec((SUBLANES, 2 * n1, c), lambda j, b: (j, 0, 0)),
                  pl.BlockSpec(m_fwd.shape, lambda j, b: (0, 0)), pl.BlockSpec(m_inv.shape, lambda j, b: (0, 0))],
        out_specs=blk,
        out_shape=jax.ShapeDtypeStruct(a5.shape, BF16),
        compiler_params=_SEQ2,
        name="fft_mid",
    )(a5, h, m_fwd, m_inv)


def _fft_filter_mid_kernel(a_ref, mf_ref, o_ref):
    for kk in range(SUBLANES):
        a = jnp.concatenate([a_ref[0, kk], a_ref[1, kk]], axis=0)
        o_ref[kk] = _dot(mf_ref[...], a)


def fft_filter_mid(a5, m_fwd):
    no, _, k2p, n1, c = a5.shape
    return pl.pallas_call(
        _fft_filter_mid_kernel,
        grid=(no, k2p // SUBLANES),
        in_specs=[pl.BlockSpec((None, 2, SUBLANES, n1, c), lambda o, j: (o, 0, j, 0, 0)),
                  pl.BlockSpec(m_fwd.shape, lambda o, j: (0, 0))],
        out_specs=pl.BlockSpec((None, SUBLANES, 2 * n1, c), lambda o, j: (o, j, 0, 0)),
        out_shape=jax.ShapeDtypeStruct((no, k2p, 2 * n1, c), F32),
        compiler_params=_SEQ2,
        name="fft_filter_mid",
    )(a5, m_fwd)


def _fft_stage3_kernel(b_ref, f_ref, u_ref, g_ref, skip_ref, o_ref, scr_ref, *, width):
    slabs = width // LANES
    slab_rows = 2 * FFT_K2_PAD * FFT_B_PITCH

    def row(r, carry):
        for part in range(2):
            tile = b_ref[part, r].astype(F32)
            for c in range(slabs):
                dst = pl.multiple_of(c * slab_rows + (part * FFT_K2_PAD + r) * FFT_B_PITCH, SUBLANES)
                scr_ref[pl.ds(dst, FFT_COLS), :] = tile[:, c * LANES:(c + 1) * LANES]
        return carry

    lax.fori_loop(0, FFT_K2_PAD, row, 0)
    for g in range(FFT_COLS):
        cols = slice(g * width, (g + 1) * width)
        spec = jnp.concatenate([scr_ref[pl.ds(c * slab_rows + g, 2 * FFT_K2_PAD, stride=FFT_B_PITCH), :]
                                for c in range(slabs)], axis=1).astype(BF16)
        y = _dot(f_ref[g], spec)
        o_ref[:, cols] = g_ref[:, cols] * (y + skip_ref[...] * u_ref[:, cols])


def fft_stage3(b5, f3, uv, gv, skip, *, width):
    nb, rows, cols = uv.shape
    blk = FFT_COLS * width
    view = pl.BlockSpec((None, rows, blk), lambda b, j: (b, 0, j))
    return pl.pallas_call(
        functools.partial(_fft_stage3_kernel, width=width),
        grid=(nb, FFT_N1 // FFT_COLS),
        in_specs=[pl.BlockSpec((None, 2, FFT_K2_PAD, FFT_COLS, width), lambda b, j: (b, 0, 0, j, 0)),
                  pl.BlockSpec((FFT_COLS, rows, 2 * FFT_K2_PAD), lambda b, j: (j, 0, 0)),
                  view, view, pl.BlockSpec((1, width), lambda b, j: (0, 0))],
        out_specs=view,
        out_shape=jax.ShapeDtypeStruct(uv.shape, F32),
        scratch_shapes=[pltpu.VMEM((width // LANES * 2 * FFT_K2_PAD * FFT_B_PITCH, LANES), F32)],
        compiler_params=_SEQ2,
        name="fft_stage3",
    )(b5, f3, uv, gv, skip.reshape(1, width))


def long_conv(uv, gv, h, skip, consts, *, width):
    f1, f3, m_fwd, m_inv = consts
    b5 = fft_mid(fft_stage1(uv, f1, width=width), h, m_fwd, m_inv)
    return fft_stage3(b5, f3, uv, gv, skip, width=width)


def filter_features(n):
    t = jnp.linspace(0.0, 1.0, n, dtype=F32)[:, None]
    w = (2.0 * math.pi / n) * jnp.arange(n, dtype=F32)[:, None]
    f = jnp.linspace(1e-4, HY_BANDS - 1, HY_BANDS, dtype=F32)[None, :]
    z = jnp.concatenate([t, jnp.cos(f * w), -jnp.sin(f * w)], axis=-1)
    z2 = jnp.concatenate([z, z[:1], z[:0:-1]], axis=0)
    return jnp.pad(z2, ((0, 0), (0, LANES - z2.shape[1])))


def _filter_mlp_kernel(z_ref, w1_ref, b1_ref, f1_ref, w2_ref, b2_ref, f2_ref, o_ref):
    a = jnp.sin(f1_ref[...] * (_dot(z_ref[...].astype(BF16), w1_ref[...].astype(BF16)) + b1_ref[...]))
    o_ref[...] = jnp.sin(f2_ref[...] * (_dot(a.astype(BF16), w2_ref[...].astype(BF16)) + b2_ref[...]))


def filter_mlp(z2, w1, b1, f1, w2, b2, f2, *, tr):
    rows = z2.shape[0]
    hid = w2.shape[0]
    w1p = jnp.pad(w1, ((0, LANES - w1.shape[0]), (0, 0)))
    vec = lambda v: v.reshape(1, hid)
    const = lambda i: (0, 0)
    return pl.pallas_call(
        _filter_mlp_kernel,
        grid=(rows // tr,),
        in_specs=[pl.BlockSpec((tr, LANES), lambda i: (i, 0)), pl.BlockSpec((LANES, hid), const),
                  pl.BlockSpec((1, hid), const), pl.BlockSpec((1, hid), const), pl.BlockSpec((hid, hid), const),
                  pl.BlockSpec((1, hid), const), pl.BlockSpec((1, hid), const)],
        out_specs=pl.BlockSpec((tr, hid), lambda i: (i, 0)),
        out_shape=jax.ShapeDtypeStruct((rows, hid), F32),
        name="filter_mlp",
    )(z2, w1p, vec(b1), vec(f1), w2, vec(b2), vec(f2))


def _filter_kernel(a_ref, t_ref, wf_ref, wb_ref, df_ref, db_ref, o_ref):
    n = a_ref.shape[0] // 2
    a = a_ref[...].astype(BF16)
    t = t_ref[...]
    hf = _dot(a[:n], wf_ref[...].astype(BF16)) * jnp.exp(-t[:n] * jnp.abs(df_ref[...]))
    hb = _dot(a[n:], wb_ref[...].astype(BF16)) * jnp.exp(-t[n:] * jnp.abs(db_ref[...]))
    row = lax.broadcasted_iota(jnp.int32, hb.shape, 0)
    hb0 = hb[0:1]
    hb = jnp.where(row == 0, 0.0, hb)
    norm = (jnp.sum(jnp.abs(hf), axis=0, keepdims=True) + jnp.sum(jnp.abs(hb), axis=0, keepdims=True)
            + jnp.abs(hb0) + 1e-6)
    hf = jnp.where(row == 0, hf + hb0, hf)
    o_ref[0:n, :] = hf / norm
    o_ref[n:, :] = hb / norm


def hyena_kernels(a2, z2, w3, decay, *, width, tc):
    rows, hid = a2.shape
    w3r = w3.reshape(hid, HY_ORDER, 2, width).transpose(1, 2, 0, 3)
    dec = decay.reshape(HY_ORDER, 2, 1, width)
    tcol = z2[:, 0:1]
    wspec = lambda s: pl.BlockSpec((None, None, hid, tc), lambda o, c: (o, s, 0, c))
    dspec = lambda s: pl.BlockSpec((None, None, 1, tc), lambda o, c: (o, s, 0, c))
    return pl.pallas_call(
        _filter_kernel,
        grid=(HY_ORDER, width // tc),
        in_specs=[pl.BlockSpec((rows, hid), lambda o, c: (0, 0)), pl.BlockSpec((rows, 1), lambda o, c: (0, 0)),
                  wspec(0), wspec(1), dspec(0), dspec(1)],
        out_specs=pl.BlockSpec((None, rows, tc), lambda o, c: (o, 0, c)),
        out_shape=jax.ShapeDtypeStruct((HY_ORDER, rows, width), F32),
        compiler_params=_SEQ2,
        name="hyena_filters",
    )(a2, tcol, w3r, w3r, dec, dec)


def filter_spectra(p, consts, *, n, width):
    f1, _, m_fwd, _ = consts
    z2 = filter_features(n)
    a2 = filter_mlp(z2, p['w1'], p['b1'], p['f1'], p['w2'], p['b2'], p['f2'], tr=1024)
    kc = hyena_kernels(a2, z2, p['w3'], p['decay'], width=width, tc=min(HY_FILTER_TILE, width))
    kv = kc.reshape(HY_ORDER, FFT_N2, FFT_N1 * width)
    return fft_filter_mid(fft_stage1(kv, f1, width=width), m_fwd)


def _hy_in_kernel(x_ref, xp_ref, xn_ref, sh_ref, sc_ref, g_ref, w_ref, cw_ref, cb_ref, v_ref, x1_ref, x2_ref, scr_ref):
    j = pl.program_id(1)
    nt = pl.num_programs(1)
    tm = x_ref.shape[0]
    width = w_ref.shape[1] // 3
    slabs = width // LANES
    blocks = tm // FFT_N1
    slab_rows = blocks * HY_VIEW_PITCH
    mod =lambda x: (_rms(x, g_ref[...]) * (1.0 + sc_ref[...]) + sh_ref[...]).astype(BF16)
    h = mod(x_ref[...])
    h_prev = mod(xp_ref[...])
    h_next = mod(xn_ref[...])
    keep_prev = jnp.where(j > 0, 1.0, 0.0)
    keep_next = jnp.where(j < nt - 1, 1.0, 0.0)
    row = lax.broadcasted_iota(jnp.int32, (tm, width), 0)
    cw = cw_ref[...]
    for part, o_ref in enumerate((v_ref, x1_ref, x2_ref)):
        cols = slice(part * width, (part + 1) * width)
        w = w_ref[:, cols]
        p = _dot(h, w)
        p_prev = _dot(h_prev, w)[SUBLANES - 1:SUBLANES] * keep_prev
        p_next = _dot(h_next, w)[0:1] * keep_next
        before = jnp.where(row == 0, p_prev, pltpu.roll(p, 1, 0))
        after = jnp.where(row == tm - 1, p_next, pltpu.roll(p, tm - 1, 0))
        res = cw[0:1, cols] * before + cw[1:2, cols] * p + cw[2:3, cols] * after + cb_ref[:, cols]
        for c in range(slabs):
            for blk in range(blocks):
                scr_ref[pl.ds(c * slab_rows + blk * HY_VIEW_PITCH, FFT_N1), :] = (
                    res[blk * FFT_N1:(blk + 1) * FFT_N1, c * LANES:(c + 1) * LANES])
        for n1 in range(FFT_N1):
            for c in range(slabs):
                lanes = slice(n1 * width + c * LANES, n1 * width + (c + 1) * LANES)
                o_ref[:, lanes] = scr_ref[pl.ds(c * slab_rows + n1, blocks, stride=HY_VIEW_PITCH), :]


def hy_in_proj(x, shift, scale, norm_g, w_in, conv_w, conv_b, *, n, tm):
    nb, _, d = x.shape
    width = w_in.shape[1] // 3
    rb = tm // SUBLANES
    last = n // SUBLANES - 1
    blocks = tm // FFT_N1
    tok = lambda b, j: (b, j, 0)
    mod_map = lambda b, j: (b, 0, 0)
    const = lambda b, j: (0, 0)
    out = pl.BlockSpec((None, blocks, FFT_N1 * width), tok)
    return pl.pallas_call(
        _hy_in_kernel,
        grid=(nb, n // tm),
        in_specs=[pl.BlockSpec((None, tm, d), tok),
                  pl.BlockSpec((None, SUBLANES, d), lambda b, j: (b, jnp.maximum(j * rb - 1, 0), 0)),
                  pl.BlockSpec((None, SUBLANES, d), lambda b, j: (b, jnp.minimum((j + 1) * rb, last), 0)),
                  pl.BlockSpec((None, 1, d), mod_map), pl.BlockSpec((None, 1, d), mod_map),
                  pl.BlockSpec((1, d), const), pl.BlockSpec(w_in.shape, const),
                  pl.BlockSpec(conv_w.shape, const), pl.BlockSpec((1, 3 * width), const)],
        out_specs=[out, out, out],
        out_shape=[jax.ShapeDtypeStruct((nb, n // FFT_N1, FFT_N1 * width), F32)] * 3,
        scratch_shapes=[pltpu.VMEM((width // LANES * blocks * HY_VIEW_PITCH, LANES), F32)],
        compiler_params=_SEQ2,
        name="hy_in_proj",
    )(x, x, x, shift, scale, norm_g.reshape(1, d), w_in.astype(BF16), conv_w, conv_b.reshape(1, -1))


def _hy_out_kernel(x_ref, gate_ref, z_ref, w_ref, o_ref, scr_ref):
    width = w_ref.shape[0]
    slabs = width // LANES
    blocks = z_ref.shape[0]
    slab_rows = blocks * HY_VIEW_PITCH
    for n1 in range(FFT_N1):
        for c in range(slabs):
            lanes = slice(n1 * width + c * LANES, n1 * width + (c + 1) * LANES)
            scr_ref[pl.ds(c * slab_rows + n1, blocks, stride=HY_VIEW_PITCH), :] = z_ref[:, lanes]
    z = jnp.concatenate(
        [jnp.concatenate([scr_ref[pl.ds(c * slab_rows + blk * HY_VIEW_PITCH, FFT_N1), :] for blk in range(blocks)],
                         axis=0) for c in range(slabs)], axis=1)
    o_ref[...] = x_ref[...] + gate_ref[...] * _dot(z.astype(BF16), w_ref[...])


def hy_out_proj(x, gate, zv, w_out, *, tm):
    nb, _, d = x.shape
    width = w_out.shape[0]
    n = zv.shape[1] * FFT_N1
    blocks = tm // FFT_N1
    tok = lambda b, j: (b, j, 0)
    return pl.pallas_call(
        _hy_out_kernel,
        grid=(nb, n // tm),
        in_specs=[pl.BlockSpec((None, tm, d), tok), pl.BlockSpec((None, 1, d), lambda b, j: (b, 0, 0)),
                  pl.BlockSpec((None, blocks, FFT_N1 * width), tok), pl.BlockSpec(w_out.shape, lambda b, j: (0, 0))],
        out_specs=pl.BlockSpec((None, tm, d), tok),
        out_shape=jax.ShapeDtypeStruct((nb, n, d), F32),
        scratch_shapes=[pltpu.VMEM((width // LANES * blocks * HY_VIEW_PITCH, LANES), F32)],
        compiler_params=_SEQ2,
        name="hy_out_proj",
    )(x, gate, zv, w_out.astype(BF16))


def hyena_mixer(x, shift, scale, gate, norm_g, p, *, n):
    width = p['w_in'].shape[1] // 3
    assert 2 * n == FFT_N
    consts = dft_constants()
    h = filter_spectra(p, consts, n=n, width=width)
    v, x1, x2 = hy_in_proj(x, shift, scale, norm_g, p['w_in'], p['conv_w'], p['conv_b'], n=n, tm=HY_IN_TILE)
    z = long_conv(v, x1, h[0], p['skip'][0], consts, width=width)
    z = long_conv(z, x2, h[1], p['skip'][1], consts, width=width)
    return hy_out_proj(x, gate, z, p['w_out'], tm=HY_IN_TILE)


def _final_norm_kernel(x_ref, g_ref, o_ref):
    o_ref[...] = _rms(x_ref[...], g_ref[...])


def final_norm(x, g):
    b, n, d = x.shape
    tm = 1024
    out = pl.pallas_call(
        _final_norm_kernel,
        grid=(b * n // tm,),
        in_specs=[pl.BlockSpec((tm, d), lambda i: (i, 0)), pl.BlockSpec((1, d), lambda i: (0, 0))],
        out_specs=pl.BlockSpec((tm, d), lambda i: (i, 0)),
        out_shape=jax.ShapeDtypeStruct((b * n, d), x.dtype),
        name="final_norm",
    )(x.reshape(b * n, d), g.reshape(1, d))
    return out.reshape(b, n, d)


def kernel(x, c, ctx, c_ctx, ada_w, ada_b, norm1_g, norm2_g, final_g, ev_w_in, mla_q_norm_g, mla_kv_norm_g, mla_w_uq, mla_w_ukv, lru_conv_w, lru_conv_b, lru_w_a, lru_b_a, lru_w_x, lru_b_x, lru_lambda, ev_w_out, od_w_in, hy_conv_w, hy_conv_b, hy_w1, hy_b1, hy_freq1, hy_w2, hy_b2, hy_freq2, hy_w3, hy_decay, hy_skip, od_w_out, router_w, router_b, moe_w1, moe_w3, moe_w2):
    nb, n_lat, d = x.shape
    n_ctx = ctx.shape[1]
    n_ctx_tiles = n_ctx // TOKEN_TILE
    cond = jnp.concatenate([jax.nn.silu(c), jax.nn.silu(c_ctx)[None, :]], axis=0)
    cond = jnp.pad(cond, ((0, 2 * SUBLANES - nb - 1), (0, 0)))

    def mod_rows(layer):
        mod = rows_matmul(cond, ada_w[layer], ada_b[layer], tn=D_MODEL)[:nb + 1]
        return [m[:, None, :] for m in jnp.split(mod, 6, axis=-1)]

    sh1, sc1, g1, sh2, sc2, g2 = mod_rows(0)
    x_all = jnp.concatenate([x, ctx], axis=1)
    p0 = dict(w_in=ev_w_in[0], q_norm_g=mla_q_norm_g[0], kv_norm_g=mla_kv_norm_g[0], w_uq=mla_w_uq[0],
              w_ukv=mla_w_ukv[0], conv_w=lru_conv_w[0], conv_b=lru_conv_b[0], w_a=lru_w_a[0], b_a=lru_b_a[0],
              w_x=lru_w_x[0], b_x=lru_b_x[0], lam=lru_lambda[0], w_out=ev_w_out[0])
    x_all = even_mixer(x_all, sh1, sc1, g1, norm1_g[0], p0, n_lat=n_lat, n_ctx=n_ctx)
    x_all = sparse_moe(x_all, sh2, sc2, g2, norm2_g[0], router_w, router_b, moe_w1[0], moe_w3[0], moe_w2[0],
                       n_ctx_tiles=n_ctx_tiles)

    sh1, sc1, g1, sh2, sc2, g2 = [m[:nb] for m in mod_rows(1)]
    p1 = dict(w_in=od_w_in[0], conv_w=hy_conv_w[0], conv_b=hy_conv_b[0], w1=hy_w1[0], b1=hy_b1[0], f1=hy_freq1[0],
              w2=hy_w2[0], b2=hy_b2[0], f2=hy_freq2[0], w3=hy_w3[0], decay=hy_decay[0], skip=hy_skip[0],
              w_out=od_w_out[0])
    x = hyena_mixer(x_all, sh1, sc1, g1, norm1_g[1], p1, n=n_lat)
    x = sparse_moe(x, sh2, sc2, g2, norm2_g[1], router_w, router_b, moe_w1[1], moe_w3[1], moe_w2[1], n_ctx_tiles=0)
    return final_norm(x, final_g)
```

```python
import functools
import math

import jax
import jax.numpy as jnp
import numpy as np
from jax import lax
from jax.experimental import pallas as pl
from jax.experimental.pallas import tpu as pltpu

D_MODEL = 1024
DEPTH = 2
GRID_W = 64
RMS_EPS = 1e-6

MLA_HEADS = 8
MLA_Q_LORA = 384
MLA_KV_LORA = 256
MLA_NOPE = 64
MLA_ROPE = 32
MLA_V = 64
MLA_QK = MLA_NOPE + MLA_ROPE
ROPE_PAIRS = MLA_ROPE // 4
ROPE_BASE = 10000.0

LRU_WIDTH = 512
LRU_BLOCKS = 8
LRU_BLOCK_DIM = LRU_WIDTH // LRU_BLOCKS
LRU_C = 8.0
LRU_CONV = 4

HY_WIDTH = D_MODEL
HY_ORDER = 2
HY_BANDS = 16
HY_HIDDEN = 64

N_EXPERTS = 16
N_GROUPS = 4
EXPERTS_PER_GROUP = N_EXPERTS // N_GROUPS

LANES = 128
SUBLANES = 8
HEAD_PAD = LANES

TOKEN_TILE = 256
MOE_ROW_TILE = 512
ATTN_Q_TILE = 1024
ATTN_KV_CHUNK = 1024
LRU_CHUNK = 256
LRU_SCAN_UNROLL = 8
DMA_LOOP_UNROLL = 8

BF16 = jnp.bfloat16
F32 = jnp.float32
_HI = lax.Precision.HIGHEST
_NEG_INF = float('-inf')


def _dot(a, b):
    return jnp.dot(a, b, preferred_element_type=F32)


def _rms(x, g):
    return x * lax.rsqrt(jnp.mean(x * x, axis=-1, keepdims=True) + RMS_EPS) * g


def _mod_map(nb, nt, n_ctx_tiles):
    if n_ctx_tiles:
        return lambda b, j: (jnp.where(j >= nt - n_ctx_tiles, nb, b), 0, 0)
    return lambda b, j: (b, 0, 0)


_SEQ2 = pltpu.CompilerParams(dimension_semantics=("arbitrary", "arbitrary"))


def _rows_matmul_kernel(x_ref, w_ref, b_ref, o_ref):
    o_ref[...] = _dot(x_ref[...].astype(BF16), w_ref[...].astype(BF16)) + b_ref[...]


def rows_matmul(x, w, b, *, tn):
    m, k = x.shape
    n = w.shape[1]
    return pl.pallas_call(
        _rows_matmul_kernel,
        grid=(n // tn,),
        in_specs=[pl.BlockSpec((m, k), lambda j: (0, 0)),
                  pl.BlockSpec((k, tn), lambda j: (0, j)),
                  pl.BlockSpec((1, tn), lambda j: (0, j))],
        out_specs=pl.BlockSpec((m, tn), lambda j: (0, j)),
        out_shape=jax.ShapeDtypeStruct((m, n), F32),
        name="rows_matmul",
    )(x, w, b.reshape(1, n))


def _rot_cols(w):
    p = ROPE_PAIRS
    return jnp.concatenate([-w[:, p:2 * p], w[:, 0:p], -w[:, 3 * p:4 * p], w[:, 2 * p:3 * p]], axis=1)


def _head_pad_cols(w, width):
    k = w.shape[0]
    w = w.reshape(k, MLA_HEADS, width)
    return jnp.pad(w, ((0, 0), (0, 0), (0, HEAD_PAD - width))).reshape(k, MLA_HEADS * HEAD_PAD)


def prep_even_weights(w_in, w_uq, w_ukv, w_out):
    d = w_in.shape[0]
    s1 = MLA_Q_LORA + MLA_KV_LORA
    s2 = s1 + MLA_ROPE
    w_kr = w_in[:, s1:s2]
    place = lambda w: jnp.pad(w, ((0, 0), (MLA_NOPE, HEAD_PAD - MLA_QK)))
    w_big = jnp.concatenate([w_in[:, :s1], w_in[:, s2:], place(w_kr), place(_rot_cols(w_kr))], axis=1).astype(BF16)
    uq = w_uq.reshape(MLA_Q_LORA, MLA_HEADS, MLA_QK)
    wq_a = _head_pad_cols(w_uq, MLA_QK)
    uq_rot = jnp.stack([_rot_cols(uq[:, h, MLA_NOPE:]) for h in range(MLA_HEADS)], axis=1)
    wq_b = jnp.pad(uq_rot, ((0, 0), (0, 0), (MLA_NOPE, HEAD_PAD - MLA_QK))).reshape(MLA_Q_LORA, MLA_HEADS * HEAD_PAD)
    ukv = w_ukv.reshape(MLA_KV_LORA, MLA_HEADS, MLA_NOPE + MLA_V)
    wk = _head_pad_cols(ukv[:, :, :MLA_NOPE].reshape(MLA_KV_LORA, -1), MLA_NOPE)
    wv = _head_pad_cols(ukv[:, :, MLA_NOPE:].reshape(MLA_KV_LORA, -1), MLA_V)
    w_kv = jnp.concatenate([wk, wv], axis=1)
    att_rows = MLA_HEADS * MLA_V
    wo_att = jnp.pad(w_out[:att_rows].reshape(MLA_HEADS, MLA_V, d),
                     ((0, 0), (0, HEAD_PAD - MLA_V), (0, 0))).reshape(MLA_HEADS * HEAD_PAD, d)
    return dict(w_big=w_big, wq_a=wq_a.astype(BF16), wq_b=wq_b.astype(BF16), w_kv=w_kv.astype(BF16),
                wo_att=wo_att.astype(BF16), wo_lru=w_out[att_rows:].astype(BF16))


def rope_tables(n_lat, n_ctx):
    rows = n_lat // GRID_W
    row = jnp.repeat(jnp.arange(rows), GRID_W)
    col = jnp.tile(jnp.arange(GRID_W), rows)
    inv_freq = ROPE_BASE ** (-jnp.arange(ROPE_PAIRS, dtype=F32) / ROPE_PAIRS)
    ang = jnp.stack([row, col], axis=-1).astype(F32)[:, :, None] * inv_freq
    cos, sin = jnp.cos(ang), jnp.sin(ang)
    c32 = jnp.concatenate([cos[:, 0], cos[:, 0], cos[:, 1], cos[:, 1]], axis=-1)
    s32 = jnp.concatenate([sin[:, 0], sin[:, 0], sin[:, 1], sin[:, 1]], axis=-1)
    c32 = jnp.concatenate([c32, jnp.ones((n_ctx, MLA_ROPE), F32)], axis=0)
    s32 = jnp.concatenate([s32, jnp.zeros((n_ctx, MLA_ROPE), F32)], axis=0)
    n = n_lat + n_ctx
    pad_hi = jnp.zeros((n, HEAD_PAD - MLA_QK), F32)
    scale = MLA_QK ** -0.5
    cq = jnp.concatenate([jnp.full((n, MLA_NOPE), scale, F32), c32 * scale, pad_hi], axis=-1)
    sq = jnp.concatenate([jnp.zeros((n, MLA_NOPE), F32), s32 * scale, pad_hi], axis=-1)
    ck = jnp.concatenate([jnp.zeros((n, MLA_NOPE), F32), c32, pad_hi], axis=-1)
    sk = jnp.concatenate([jnp.zeros((n, MLA_NOPE), F32), s32, pad_hi], axis=-1)
    return cq, sq, ck, sk


def _even_in_kernel(x_ref, sh_ref, sc_ref, g_ref, wbig_ref, qg_ref, kvg_ref, wqa_ref, wqb_ref, wkv_ref,
                    cq_ref, sq_ref, ck_ref, sk_ref, q_ref, k_ref, v_ref, ux_ref, ug_ref):
    h = (_rms(x_ref[...], g_ref[...]) * (1.0 + sc_ref[...]) + sh_ref[...]).astype(BF16)
    p = _dot(h, wbig_ref[...])
    o1 = MLA_Q_LORA
    o2 = o1 + MLA_KV_LORA
    o3 = o2 + LRU_WIDTH
    o4 = o3 + LRU_WIDTH
    o5 = o4 + HEAD_PAD
    ux_ref[...] = p[:, o2:o3]
    ug_ref[...] = p[:, o3:o4]
    cqn = _rms(p[:, :o1], qg_ref[...]).astype(BF16)
    rep = lambda t: jnp.concatenate([t] * MLA_HEADS, axis=-1)
    q = _dot(cqn, wqa_ref[...]) * rep(cq_ref[...]) + _dot(cqn, wqb_ref[...]) * rep(sq_ref[...])
    q_ref[...] = q.astype(BF16)
    ckvn = _rms(p[:, o1:o2], kvg_ref[...]).astype(BF16)
    kv = _dot(ckvn, wkv_ref[...])
    k_rope = p[:, o4:o5] * ck_ref[...] + p[:, o5:] * sk_ref[...]
    hw = MLA_HEADS * HEAD_PAD
    k_ref[...] = (kv[:, :hw] + rep(k_rope)).astype(BF16)
    lane = lax.broadcasted_iota(jnp.int32, (1, hw), 1)
    ones_col = jnp.where((lane & (HEAD_PAD - 1)) == MLA_V, 1.0, 0.0)
    v_ref[...] = (kv[:, hw:] + ones_col).astype(BF16)


def even_in_proj(x, shift, scale, norm_g, wts, q_norm_g, kv_norm_g, tables, *, tm, n_ctx_tiles):
    nb, s_len, d = x.shape
    nt = s_len // tm
    hw = MLA_HEADS * HEAD_PAD
    mod_map = _mod_map(nb, nt, n_ctx_tiles)
    const = lambda b, j: (0, 0)
    tok = lambda b, j: (b, j, 0)
    tab = lambda b, j: (j, 0)
    full = lambda a: pl.BlockSpec(a.shape, const)
    row = lambda v: v.reshape(1, -1)
    args = [x, shift, scale, row(norm_g), wts['w_big'], row(q_norm_g), row(kv_norm_g),
            wts['wq_a'], wts['wq_b'], wts['w_kv']]
    in_specs = [pl.BlockSpec((None, tm, d), tok), pl.BlockSpec((None, 1, d), mod_map),
                pl.BlockSpec((None, 1, d), mod_map)]
    in_specs += [full(a) for a in args[3:]]
    in_specs += [pl.BlockSpec((tm, HEAD_PAD), tab)] * 4
    out_dims = [(hw, BF16), (hw, BF16), (hw, BF16), (LRU_WIDTH, F32), (LRU_WIDTH, F32)]
    return pl.pallas_call(
        _even_in_kernel,
        grid=(nb, nt),
        in_specs=in_specs,
        out_specs=[pl.BlockSpec((None, tm, w), tok) for w, _ in out_dims],
        out_shape=[jax.ShapeDtypeStruct((nb, s_len, w), dt) for w, dt in out_dims],
        compiler_params=_SEQ2,
        name="even_in_proj",
    )(*args, *tables)


def _attn_kernel(q_ref, k_ref, v_ref, o_ref):
    q = q_ref[...]
    tq = q.shape[0]
    n_k = k_ref.shape[0]
    m = jnp.full((tq, 1), _NEG_INF, F32)
    acc = jnp.zeros((tq, HEAD_PAD), F32)
    for start in range(0, n_k, ATTN_KV_CHUNK):
        size = min(ATTN_KV_CHUNK, n_k - start)
        s = lax.dot_general(q, k_ref[start:start + size, :], (((1,), (1,)), ((), ())), preferred_element_type=F32)
        m_new = jnp.maximum(m, jnp.max(s, axis=-1, keepdims=True))
        p = jnp.exp(s - m_new).astype(BF16)
        acc = jnp.exp(m - m_new) * acc + _dot(p, v_ref[start:start + size, :])
        m = m_new
    o_ref[...] = (acc / acc[:, MLA_V:MLA_V + 1]).astype(o_ref.dtype)


def attention(q, k, v, *, q_start, n_q, k_start, n_k, tq):
    nb = q.shape[0]
    qo = q_start // tq
    ko = k_start // n_k
    return pl.pallas_call(
        _attn_kernel,
        grid=(nb, MLA_HEADS, n_q // tq),
        in_specs=[pl.BlockSpec((None, tq, HEAD_PAD), lambda b, h, i: (b, qo + i, h)),
                  pl.BlockSpec((None, n_k, HEAD_PAD), lambda b, h, i: (b, ko, h)),
                  pl.BlockSpec((None, n_k, HEAD_PAD), lambda b, h, i: (b, ko, h))],
        out_specs=pl.BlockSpec((None, tq, HEAD_PAD), lambda b, h, i: (b, i, h)),
        out_shape=jax.ShapeDtypeStruct((nb, n_q, MLA_HEADS * HEAD_PAD), BF16),
        compiler_params=pltpu.CompilerParams(dimension_semantics=("arbitrary", "arbitrary", "arbitrary")),
        name="attention",
    )(q, k, v)


def _scan_block(a, b, h_prev, row, reverse):
    for s in (1, 2, 4):
        sh = SUBLANES - s if reverse else s
        a_s = pltpu.roll(a, sh, 0)
        b_s = pltpu.roll(b, sh, 0)
        ok = (row < SUBLANES - s) if reverse else (row >= s)
        b = jnp.where(ok, a * b_s + b, b)
        a = jnp.where(ok, a * a_s, a)
    return a * h_prev + b


def _rglru_kernel(ux_ref, ug_ref, cw_ref, cb_ref, wa_ref, ba_ref, wx_ref, bx_ref, lam_ref, o_ref,
                  xp_ref, a_ref, b_ref, hf_ref, hb_ref, *, n_lat, n_ctx):
    lanes = ux_ref.shape[-1]
    pad = SUBLANES
    zeros = jnp.zeros((pad, lanes), F32)
    lat0 = pad
    ctx0 = 2 * pad + n_lat
    xp_ref[0:pad, :] = zeros
    xp_ref[lat0 + n_lat:ctx0, :] = zeros
    xp_ref[ctx0 + n_ctx:ctx0 + n_ctx + pad, :] = zeros
    xp_ref[lat0:lat0 + n_lat, :] = ux_ref[0:n_lat, :]
    xp_ref[ctx0:ctx0 + n_ctx, :] = ux_ref[n_lat:n_lat + n_ctx, :]
    cw = cw_ref[...]
    n_chunks = (n_lat + n_ctx) // LRU_CHUNK
    n_lat_chunks = n_lat // LRU_CHUNK
    n_win = LRU_CHUNK + 2 * pad

    def coeffs(c, carry):
        src = pl.multiple_of(c * LRU_CHUNK + jnp.where(c >= n_lat_chunks, pad, 0), SUBLANES)
        dst = pl.multiple_of(c * LRU_CHUNK, SUBLANES)
        win = xp_ref[pl.ds(src, n_win), :]
        mid = lambda t: t[pad:pad + LRU_CHUNK]
        u = cb_ref[...] + cw[2:3, :] * mid(win)
        u = u + cw[0:1, :] * mid(pltpu.roll(win, 2, 0))
        u = u + cw[1:2, :] * mid(pltpu.roll(win, 1, 0))
        u = u + cw[3:4, :] * mid(pltpu.roll(win, n_win - 1, 0))
        ub = u.astype(BF16)
        for d in range(2):
            r = jax.nn.sigmoid(_dot(ub, wa_ref[d]) + ba_ref[d])
            i = jax.nn.sigmoid(_dot(ub, wx_ref[d]) + bx_ref[d])
            a = jnp.exp(lam_ref[d] * r)
            a_ref[d, pl.ds(dst, LRU_CHUNK), :] = a
            b_ref[d, pl.ds(dst, LRU_CHUNK), :] = jnp.sqrt(1.0 - a * a) * (i * u)
        return carry

    lax.fori_loop(0, n_chunks, coeffs, 0)
    row = lax.broadcasted_iota(jnp.int32, (SUBLANES, lanes), 0)

    def make_step(first_blk, n_blk):
        def step(t, carry):
            h_f, h_b = carry
            rf = pl.multiple_of((first_blk + t) * SUBLANES, SUBLANES)
            rb = pl.multiple_of((first_blk + n_blk - 1 - t) * SUBLANES, SUBLANES)
            out_f = _scan_block(a_ref[0, pl.ds(rf, SUBLANES), :], b_ref[0, pl.ds(rf, SUBLANES), :], h_f, row, False)
            out_b = _scan_block(a_ref[1, pl.ds(rb, SUBLANES), :], b_ref[1, pl.ds(rb, SUBLANES), :], h_b, row, True)
            hf_ref[pl.ds(rf, SUBLANES), :] = out_f
            hb_ref[pl.ds(rb, SUBLANES), :] = out_b
            h_f = jnp.broadcast_to(out_f[SUBLANES - 1:SUBLANES, :], out_f.shape)
            h_b = jnp.broadcast_to(out_b[0:1, :], out_b.shape)
            return h_f, h_b
        return step

    state = (jnp.zeros((SUBLANES, lanes), F32), jnp.zeros((SUBLANES, lanes), F32))
    state = lax.fori_loop(0, n_ctx // SUBLANES, make_step(n_lat // SUBLANES, n_ctx // SUBLANES), state,
                          unroll=LRU_SCAN_UNROLL)
    lax.fori_loop(0, n_lat // SUBLANES, make_step(0, n_lat // SUBLANES), state, unroll=LRU_SCAN_UNROLL)

    def gate(c, carry):
        r0 = pl.multiple_of(c * LRU_CHUNK, SUBLANES)
        g = ug_ref[pl.ds(r0, LRU_CHUNK), :]
        gelu = 0.5 * g * (1.0 + jnp.tanh(math.sqrt(2.0 / math.pi) * (g + 0.044715 * (g * g * g))))
        y = hf_ref[pl.ds(r0, LRU_CHUNK), :] + hb_ref[pl.ds(r0, LRU_CHUNK), :]
        o_ref[pl.ds(r0, LRU_CHUNK), :] = (y * gelu).astype(o_ref.dtype)
        return carry

    lax.fori_loop(0, n_chunks, gate, 0)


def _block_diag(w):
    per = LANES // LRU_BLOCK_DIM
    nt = LRU_BLOCKS // per
    w = w.reshape(2, nt, per, LRU_BLOCK_DIM, LRU_BLOCK_DIM)
    eye = jnp.eye(per, dtype=w.dtype)
    out = jnp.einsum('dtpij,pq->dtpiqj', w, eye)
    return out.reshape(2, nt, LANES, LANES)


def rglru(ux, ug, conv_w, conv_b, w_a, b_a, w_x, b_x, lam, *, n_lat, n_ctx):
    nb, s_len, width = ux.shape
    nt = width // LANES
    wa = _block_diag(w_a).astype(BF16).transpose(1, 0, 2, 3)
    wx = _block_diag(w_x).astype(BF16).transpose(1, 0, 2, 3)
    lam_c = -LRU_C * jax.nn.softplus(-lam.astype(F32))
    lane3 = lambda v: v.reshape(2, 1, width)
    seq = pl.BlockSpec((None, s_len, LANES), lambda b, c: (b, 0, c))
    vec = lambda rows: pl.BlockSpec((rows, LANES), lambda b, c: (0, c))
    dvec = pl.BlockSpec((2, 1, LANES), lambda b, c: (0, 0, c))
    wspec = pl.BlockSpec((None, 2, LANES, LANES), lambda b, c: (c, 0, 0, 0))
    return pl.pallas_call(
        functools.partial(_rglru_kernel, n_lat=n_lat, n_ctx=n_ctx),
        grid=(nb, nt),
        in_specs=[seq, seq, vec(LRU_CONV), vec(1), wspec, dvec, wspec, dvec, dvec],
        out_specs=seq,
        out_shape=jax.ShapeDtypeStruct((nb, s_len, width), BF16),
        scratch_shapes=[pltpu.VMEM((s_len + 3 * SUBLANES, LANES), F32),
                        pltpu.VMEM((2, s_len, LANES), F32), pltpu.VMEM((2, s_len, LANES), F32),
                        pltpu.VMEM((s_len, LANES), F32), pltpu.VMEM((s_len, LANES), F32)],
        compiler_params=_SEQ2,
        name="rglru",
    )(ux, ug, conv_w, conv_b.reshape(1, width), wa, lane3(b_a), wx, lane3(b_x), lane3(lam_c))


def _even_out_kernel(x_ref, gate_ref, att_ref, lru_ref, wa_ref, wl_ref, o_ref):
    y = _dot(att_ref[...], wa_ref[...]) + _dot(lru_ref[...], wl_ref[...])
    o_ref[...] = x_ref[...] + gate_ref[...] * y


def even_out_proj(x, gate, att, lru, wo_att, wo_lru, *, tm, n_ctx_tiles):
    nb, s_len, d = x.shape
    nt = s_len // tm
    tok = lambda b, j: (b, j, 0)
    const = lambda b, j: (0, 0)
    return pl.pallas_call(
        _even_out_kernel,
        grid=(nb, nt),
        in_specs=[pl.BlockSpec((None, tm, d), tok), pl.BlockSpec((None, 1, d), _mod_map(nb, nt, n_ctx_tiles)),
                  pl.BlockSpec((None, tm, att.shape[-1]), tok), pl.BlockSpec((None, tm, lru.shape[-1]), tok),
                  pl.BlockSpec(wo_att.shape, const), pl.BlockSpec(wo_lru.shape, const)],
        out_specs=pl.BlockSpec((None, tm, d), tok),
        out_shape=jax.ShapeDtypeStruct((nb, s_len, d), F32),
        compiler_params=_SEQ2,
        name="even_out_proj",
    )(x, gate, att, lru, wo_att, wo_lru)


def even_mixer(x_all, shift, scale, gate, norm_g, p, *, n_lat, n_ctx):
    n_ctx_tiles = n_ctx // TOKEN_TILE
    tables = rope_tables(n_lat, n_ctx)
    wts = prep_even_weights(p['w_in'], p['w_uq'], p['w_ukv'], p['w_out'])
    q, k, v, ux, ug = even_in_proj(x_all, shift, scale, norm_g, wts, p['q_norm_g'], p['kv_norm_g'], tables,
                                   tm=TOKEN_TILE, n_ctx_tiles=n_ctx_tiles)
    att_l = attention(q, k, v, q_start=0, n_q=n_lat, k_start=0, n_k=n_lat + n_ctx, tq=ATTN_Q_TILE)
    att_c = attention(q, k, v, q_start=n_lat, n_q=n_ctx, k_start=n_lat, n_k=n_ctx, tq=n_ctx)
    att = jnp.concatenate([att_l, att_c], axis=1)
    lru = rglru(ux, ug, p['conv_w'], p['conv_b'], p['w_a'], p['b_a'], p['w_x'], p['b_x'], p['lam'],
                n_lat=n_lat, n_ctx=n_ctx)
    return even_out_proj(x_all, gate, att, lru, wts['wo_att'], wts['wo_lru'], tm=TOKEN_TILE, n_ctx_tiles=n_ctx_tiles)


def _load_token_tiles(ref, n_tok, d):
    return jnp.concatenate([ref[pl.ds(c, n_tok, stride=SUBLANES), :] for c in range(d // LANES)], axis=1)


def _store_token_tiles(ref, val):
    n_tok, d = val.shape
    for c in range(d // LANES):
        ref[pl.ds(c, n_tok, stride=SUBLANES), :] = val[:, c * LANES:(c + 1) * LANES]


def _first_argmax(vals):
    best = vals[0]
    idx = jnp.zeros(vals[0].shape, jnp.int32)
    for j in range(1, len(vals)):
        better = vals[j] > best
        idx = jnp.where(better, j, idx)
        best = jnp.where(better, vals[j], best)
    return idx, best


def _select(idx, vals):
    out = vals[0]
    for j in range(1, len(vals)):
        out = jnp.where(idx == j, vals[j], out)
    return out


def _moe_route_kernel(x_ref, sh_ref, sc_ref, g_ref, rwt_ref, rb_ref,
                      h_ref, e_ref, w_ref, rank_ref, cnt_ref, carry_ref, *, n_groups, per_group):
    first = jnp.logical_and(pl.program_id(0) == 0, pl.program_id(1) == 0)

    @pl.when(first)
    def _():
        carry_ref[...] = jnp.zeros_like(carry_ref)

    tm = x_ref.shape[0]
    n_exp = n_groups * per_group
    h = _rms(x_ref[...], g_ref[...]) * (1.0 + sc_ref[...]) + sh_ref[...]
    _store_token_tiles(h_ref, h)
    logits = lax.dot_general(rwt_ref[...], h, (((1,), (1,)), ((), ())),
                             precision=_HI, preferred_element_type=F32)
    s = jax.nn.sigmoid(logits)
    sel = s + rb_ref[...]
    sel_rows = [sel[e:e + 1, :] for e in range(n_exp)]
    s_rows = [s[e:e + 1, :] for e in range(n_exp)]
    g_scores = []
    for g in range(n_groups):
        r = sel_rows[g * per_group:(g + 1) * per_group]
        pair_sums = [r[a] + r[b] for a in range(per_group) for b in range(a + 1, per_group)]
        g_scores.append(functools.reduce(jnp.maximum, pair_sums))
    g_idx, _ = _first_argmax(g_scores)
    v = [_select(g_idx, [sel_rows[g * per_group + j] for g in range(n_groups)]) for j in range(per_group)]
    sv = [_select(g_idx, [s_rows[g * per_group + j] for g in range(n_groups)]) for j in range(per_group)]
    i1, _ = _first_argmax(v)
    i2, _ = _first_argmax([jnp.full_like(v[0], _NEG_INF)]
                          + [jnp.where(i1 == j, _NEG_INF, v[j]) for j in range(per_group)])
    i2 = i2 - 1
    w1 = _select(i1, sv)
    w2 = _select(i2, sv)
    wsum = w1 + w2
    e1 = g_idx * per_group + i1
    e2 = g_idx * per_group + i2
    e_ref[0:1, :] = e1
    e_ref[1:2, :] = e2
    w_ref[0:1, :] = w1 / wsum
    w_ref[1:2, :] = w2 / wsum
    e_iota = lax.broadcasted_iota(jnp.int32, (n_exp, tm), 0)
    oh1 = e_iota == e1
    oh2 = e_iota == e2
    m = jnp.where(oh1, 1.0, jnp.where(oh2, 1.0, 0.0))
    r_i = lax.broadcasted_iota(jnp.int32, (tm, tm), 0)
    c_i = lax.broadcasted_iota(jnp.int32, (tm, tm), 1)
    upper = jnp.where(r_i < c_i, 1.0, 0.0).astype(BF16)
    pre = _dot(m.astype(BF16), upper)
    tot = pre + carry_ref[:, 0:1]
    rank_ref[0:1, :] = jnp.sum(jnp.where(oh1, tot, 0.0), axis=0, keepdims=True).astype(jnp.int32)
    rank_ref[1:2, :] = jnp.sum(jnp.where(oh2, tot, 0.0), axis=0, keepdims=True).astype(jnp.int32)
    new_carry = carry_ref[...] + jnp.sum(m, axis=1, keepdims=True)
    carry_ref[...] = new_carry
    cnt_ref[...] = new_carry


def moe_route(x, shift, scale, norm_g, router_w, router_b, *, tm, n_ctx_tiles, n_groups, per_group):
    nb, s_len, d = x.shape
    nt = s_len // tm
    n_exp = n_groups * per_group
    rwt = router_w.T
    rb = router_b.reshape(n_exp, 1)
    mod_map = _mod_map(nb, nt, n_ctx_tiles)
    tok_map = lambda b, j: (b * nt + j, 0, 0)
    return pl.pallas_call(
        functools.partial(_moe_route_kernel, n_groups=n_groups, per_group=per_group),
        grid=(nb, nt),
        in_specs=[
            pl.BlockSpec((None, tm, d), lambda b, j: (b, j, 0)),
            pl.BlockSpec((None, 1, d), mod_map),
            pl.BlockSpec((None, 1, d), mod_map),
            pl.BlockSpec((1, d), lambda b, j: (0, 0)),
            pl.BlockSpec((n_exp, d), lambda b, j: (0, 0)),
            pl.BlockSpec((n_exp, 1), lambda b, j: (0, 0)),
        ],
        out_specs=[
            pl.BlockSpec((tm * SUBLANES, d // SUBLANES), lambda b, j: (b * nt + j, 0)),
            pl.BlockSpec((None, 2, tm), tok_map),
            pl.BlockSpec((None, 2, tm), tok_map),
            pl.BlockSpec((None, 2, tm), tok_map),
            pl.BlockSpec((n_exp, LANES), lambda b, j: (0, 0)),
        ],
        out_shape=[
            jax.ShapeDtypeStruct((nb * s_len * SUBLANES, d // SUBLANES), F32),
            jax.ShapeDtypeStruct((nb * nt, 2, tm), jnp.int32),
            jax.ShapeDtypeStruct((nb * nt, 2, tm), F32),
            jax.ShapeDtypeStruct((nb * nt, 2, tm), jnp.int32),
            jax.ShapeDtypeStruct((n_exp, LANES), F32),
        ],
        scratch_shapes=[pltpu.VMEM((n_exp, LANES), F32)],
        compiler_params=_SEQ2,
        name="moe_route",
    )(x, shift, scale, norm_g.reshape(1, d), rwt, rb)


def _row_copy(src_ref, src_tok, dst_ref, dst_tok, sem):
    src = src_ref.at[pl.ds(pl.multiple_of(src_tok * SUBLANES, SUBLANES), SUBLANES)]
    dst = dst_ref.at[pl.ds(pl.multiple_of(dst_tok * SUBLANES, SUBLANES), SUBLANES)]
    return pltpu.make_async_copy(src, dst, sem)


def _moe_dispatch_kernel(pos_ref, h_ref, xs_in_ref, xs_ref, sem):
    del xs_in_ref
    tm = h_ref.shape[0] // SUBLANES

    def start(r, c):
        _row_copy(h_ref, r, xs_ref, pos_ref[0, r], sem).start()
        _row_copy(h_ref, r, xs_ref, pos_ref[1, r], sem).start(priority=1)
        return c

    lax.fori_loop(0, tm, start, 0, unroll=DMA_LOOP_UNROLL)

    def wait(r, c):
        _row_copy(h_ref, 0, xs_ref, 0, sem).wait()
        _row_copy(h_ref, 0, xs_ref, 0, sem).wait()
        return c

    lax.fori_loop(0, tm, wait, 0, unroll=DMA_LOOP_UNROLL)


def moe_dispatch(h_tiles, pos, n_rows, *, tm):
    rows, lanes = h_tiles.shape
    zeros = jnp.zeros((n_rows * SUBLANES, lanes), h_tiles.dtype)
    return pl.pallas_call(
        _moe_dispatch_kernel,
        grid=(rows // (tm * SUBLANES),),
        in_specs=[
            pl.BlockSpec((None, 2, tm), lambda i: (i, 0, 0), memory_space=pltpu.SMEM),
            pl.BlockSpec((tm * SUBLANES, lanes), lambda i: (i, 0)),
            pl.BlockSpec(memory_space=pl.ANY),
        ],
        out_specs=pl.BlockSpec(memory_space=pl.ANY),
        out_shape=jax.ShapeDtypeStruct(zeros.shape, h_tiles.dtype),
        scratch_shapes=[pltpu.SemaphoreType.DMA(())],
        input_output_aliases={2: 0},
        compiler_params=pltpu.CompilerParams(dimension_semantics=("arbitrary",), has_side_effects=True),
        name="moe_dispatch",
    )(pos, h_tiles, zeros)


def _moe_expert_kernel(te_ref, nv_ref, xs_ref, w1_ref, w3_ref, w2_ref, ys_ref, w1b_ref, w3b_ref, w2b_ref):
    i = pl.program_id(0)
    live = i < nv_ref[0]

    @pl.when(jnp.logical_and(live, jnp.logical_or(i == 0, te_ref[i] != te_ref[jnp.maximum(i - 1, 0)])))
    def _():
        w1b_ref[...] = w1_ref[...].astype(BF16)
        w3b_ref[...] = w3_ref[...].astype(BF16)
        w2b_ref[...] = w2_ref[...].astype(BF16)

    @pl.when(live)
    def _():
        d = w1_ref.shape[0]
        x = _load_token_tiles(xs_ref, xs_ref.shape[0] // SUBLANES, d).astype(BF16)
        a = _dot(x, w1b_ref[...])
        b = _dot(x, w3b_ref[...])
        act = (a * jax.nn.sigmoid(a) * b).astype(BF16)
        _store_token_tiles(ys_ref, _dot(act, w2b_ref[...]))

    @pl.when(jnp.logical_not(live))
    def _():
        ys_ref[...] = jnp.zeros_like(ys_ref)


def moe_experts(xs, tile_expert, n_valid, w1, w3, w2, *, tmm):
    rows, lanes = xs.shape
    n_tiles = rows // (tmm * SUBLANES)
    d, ff = w1.shape[-2:]
    row_map = lambda i, te, nv: (jnp.minimum(i, nv[0] - 1), 0)
    w_map = lambda i, te, nv: (te[jnp.minimum(i, nv[0] - 1)], 0, 0)
    grid_spec = pltpu.PrefetchScalarGridSpec(
        num_scalar_prefetch=2,
        grid=(n_tiles,),
        in_specs=[
            pl.BlockSpec((tmm * SUBLANES, lanes), row_map),
            pl.BlockSpec((None, d, ff), w_map),
            pl.BlockSpec((None, d, ff), w_map),
            pl.BlockSpec((None, ff, d), w_map),
        ],
        out_specs=pl.BlockSpec((tmm * SUBLANES, lanes), lambda i, te, nv: (i, 0)),
        scratch_shapes=[pltpu.VMEM((d, ff), BF16), pltpu.VMEM((d, ff), BF16), pltpu.VMEM((ff, d), BF16)],
    )
    return pl.pallas_call(
        _moe_expert_kernel,
        grid_spec=grid_spec,
        out_shape=jax.ShapeDtypeStruct(xs.shape, F32),
        compiler_params=pltpu.CompilerParams(dimension_semantics=("arbitrary",)),
        name="moe_experts",
    )(tile_expert, n_valid, xs, w1, w3, w2)


def _moe_combine_kernel(pos_ref, x_ref, gate_ref, w_ref, fg_ref, ys_ref, o_ref, buf0_ref, buf1_ref, sem, *, final):
    tm, d = x_ref.shape

    def start(r, c):
        _row_copy(ys_ref, pos_ref[0, r], buf0_ref, r, sem).start()
        _row_copy(ys_ref, pos_ref[1, r], buf1_ref, r, sem).start(priority=1)
        return c

    lax.fori_loop(0, tm, start, 0, unroll=DMA_LOOP_UNROLL)

    def wait(r, c):
        _row_copy(ys_ref, 0, buf0_ref, 0, sem).wait()
        _row_copy(ys_ref, 0, buf1_ref, 0, sem).wait()
        return c

    lax.fori_loop(0, tm, wait, 0, unroll=DMA_LOOP_UNROLL)
    w = w_ref[...]
    y = w[:, 0:1] * _load_token_tiles(buf0_ref, tm, d) + w[:, 1:2] * _load_token_tiles(buf1_ref, tm, d)
    out = x_ref[...] + gate_ref[...] * y
    o_ref[...] = _rms(out, fg_ref[...]) if final else out


def moe_combine(x, gate, wts, pos, ys, final_g, *, tm, n_ctx_tiles, final):
    nb, s_len, d = x.shape
    nt = s_len // tm
    return pl.pallas_call(
        functools.partial(_moe_combine_kernel, final=final),
        grid=(nb, nt),
        in_specs=[
            pl.BlockSpec((None, 2, tm), lambda b, j: (b * nt + j, 0, 0), memory_space=pltpu.SMEM),
            pl.BlockSpec((None, tm, d), lambda b, j: (b, j, 0)),
            pl.BlockSpec((None, 1, d), _mod_map(nb, nt, n_ctx_tiles)),
            pl.BlockSpec((tm, 2), lambda b, j: (b * nt + j, 0)),
            pl.BlockSpec((1, d), lambda b, j: (0, 0)),
            pl.BlockSpec(memory_space=pl.ANY),
        ],
        out_specs=pl.BlockSpec((None, tm, d), lambda b, j: (b, j, 0)),
        out_shape=jax.ShapeDtypeStruct((nb, s_len, d), F32),
        scratch_shapes=[pltpu.VMEM((tm * SUBLANES, d // SUBLANES), F32), pltpu.VMEM((tm * SUBLANES, d // SUBLANES), F32),
                        pltpu.SemaphoreType.DMA(())],
        compiler_params=_SEQ2,
        name="moe_combine",
    )(pos, x, gate, wts, final_g.reshape(1, d), ys)


def sparse_moe(x, shift, scale, gate, norm_g, router_w, router_b, w1, w3, w2, final_g, *, n_ctx_tiles, final):
    nb, s_len, d = x.shape
    assert d == SUBLANES * LANES
    tm, tmm = TOKEN_TILE, MOE_ROW_TILE
    t = nb * s_len
    h, e, w, rank, cnt = moe_route(x, shift, scale, norm_g, router_w, router_b, tm=tm, n_ctx_tiles=n_ctx_tiles,
                                   n_groups=N_GROUPS, per_group=EXPERTS_PER_GROUP)
    counts = cnt[:, 0].astype(jnp.int32)
    padded = ((counts + tmm - 1) // tmm) * tmm
    ends = jnp.cumsum(padded)
    offs = ends - padded
    pos = offs[e] + rank
    n_tiles = (2 * t + N_EXPERTS * (tmm - 1)) // tmm + 1
    n_valid = (ends[-1] // tmm).astype(jnp.int32).reshape(1)
    tile_start = jnp.arange(n_tiles, dtype=jnp.int32) * tmm
    tile_expert = jnp.minimum(jnp.sum((tile_start[:, None] >= ends[None, :]).astype(jnp.int32), axis=1), N_EXPERTS - 1)
    xs = moe_dispatch(h, pos, n_tiles * tmm, tm=tm)
    ys = moe_experts(xs, tile_expert, n_valid, w1, w3, w2, tmm=tmm)
    wts = w.transpose(0, 2, 1).reshape(t, 2)
    return moe_combine(x, gate, wts, pos, ys, final_g, tm=tm, n_ctx_tiles=n_ctx_tiles, final=final)


FFT_N1 = 64
FFT_N2 = 128
FFT_N = FFT_N1 * FFT_N2
FFT_K2 = FFT_N2 // 2 + 1
FFT_K2_PAD = 72
FFT_COLS = 16
FFT_A_PITCH = 152
FFT_B_PITCH = 24
HY_VIEW_PITCH = 72
HY_IN_TILE = 512
HY_FILTER_TILE = 256


def dft_constants():
    n1 = np.arange(FFT_N1)
    n2 = np.arange(FFT_N2)
    k2 = np.arange(FFT_K2_PAD)
    n = n1[:, None, None] + FFT_N1 * n2[None, None, :]
    ang = 2.0 * np.pi * ((k2[None, :, None] * n) % FFT_N) / FFT_N
    live = (k2 < FFT_K2)[None, :, None]
    f1 = np.concatenate([np.cos(ang) * live, -np.sin(ang) * live], axis=1)
    wgt = np.where((k2 == 0) | (k2 == FFT_N2 // 2), 1.0, 2.0) * (k2 < FFT_K2) / FFT_N
    ang3 = np.transpose(ang, (0, 2, 1))
    f3 = np.concatenate([np.cos(ang3) * wgt, -np.sin(ang3) * wgt], axis=2)
    f3 = f3[:, :FFT_N2 // 2]
    a = 2.0 * np.pi * ((n1[:, None] * n1[None, :]) % FFT_N1) / FFT_N1
    cr, ci = np.cos(a), -np.sin(a)
    m_fwd = np.block([[cr, -ci], [ci, cr]])
    m_inv = np.block([[cr, ci], [-ci, cr]])
    return (jnp.asarray(f1, BF16), jnp.asarray(f3, BF16), jnp.asarray(m_fwd, BF16), jnp.asarray(m_inv, BF16))


def _fft_stage1_kernel(x_ref, f_ref, o_ref, scr_ref, *, width):
    slabs = width // LANES
    slab_rows = FFT_COLS * FFT_A_PITCH
    for g in range(FFT_COLS):
        res = _dot(f_ref[g], x_ref[:, g * width:(g + 1) * width].astype(BF16))
        for c in range(slabs):
            scr_ref[pl.ds(c * slab_rows + g * FFT_A_PITCH, 2 * FFT_K2_PAD), :] = res[:, c * LANES:(c + 1) * LANES]

    def row(r, carry):
        for part in range(2):
            for c in range(slabs):
                base = c * slab_rows + part * FFT_K2_PAD + r
                lo = scr_ref[pl.ds(base, SUBLANES, stride=FFT_A_PITCH), :]
                hi = scr_ref[pl.ds(base + SUBLANES * FFT_A_PITCH, SUBLANES, stride=FFT_A_PITCH), :]
                o_ref[part, r, :, c * LANES:(c + 1) * LANES] = jnp.concatenate([lo, hi], axis=0).astype(o_ref.dtype)
        return carry

    lax.fori_loop(0, FFT_K2_PAD, row, 0)


def fft_stage1(xv, f1, *, width):
    nb, k, cols = xv.shape
    blk = FFT_COLS * width
    return pl.pallas_call(
        functools.partial(_fft_stage1_kernel, width=width),
        grid=(nb, FFT_N1 // FFT_COLS),
        in_specs=[pl.BlockSpec((None, k, blk), lambda b, j: (b, 0, j)),
                  pl.BlockSpec((FFT_COLS, 2 * FFT_K2_PAD, k), lambda b, j: (j, 0, 0))],
        out_specs=pl.BlockSpec((None, 2, FFT_K2_PAD, FFT_COLS, width), lambda b, j: (b, 0, 0, j, 0)),
        out_shape=jax.ShapeDtypeStruct((nb, 2, FFT_K2_PAD, FFT_N1, width), BF16),
        scratch_shapes=[pltpu.VMEM((width // LANES * FFT_COLS * FFT_A_PITCH, LANES), F32)],
        compiler_params=_SEQ2,
        name="fft_stage1",
    )(xv, f1[:, :, :k])


def _fft_mid_kernel(a_ref, h_ref, mf_ref, mi_ref, o_ref):
    for kk in range(SUBLANES):
        a = jnp.concatenate([a_ref[0, kk], a_ref[1, kk]], axis=0)
        x = _dot(mf_ref[...], a)
        xr, xi = x[:FFT_N1], x[FFT_N1:]
        hr, hi = h_ref[kk, :FFT_N1], h_ref[kk, FFT_N1:]
        y = jnp.concatenate([xr * hr - xi * hi, xr * hi + xi * hr], axis=0).astype(BF16)
        b = _dot(mi_ref[...], y)
        o_ref[0, kk] = b[:FFT_N1].astype(o_ref.dtype)
        o_ref[1, kk] = b[FFT_N1:].astype(o_ref.dtype)


def fft_mid(a5, h, m_fwd, m_inv):
    nb, _, k2p, n1, c = a5.shape
    blk = pl.BlockSpec((None, 2, SUBLANES, n1, c), lambda j, b: (b, 0, j, 0, 0))
    return pl.pallas_call(
        _fft_mid_kernel,
        grid=(k2p // SUBLANES, nb),
        in_specs=[blk, pl.BlockSpec((SUBLANES, 2 * n1, c), lambda j, b: (j, 0, 0)),
                  pl.BlockSpec(m_fwd.shape, lambda j, b: (0, 0)), pl.BlockSpec(m_inv.shape, lambda j, b: (0, 0))],
        out_specs=blk,
        out_shape=jax.ShapeDtypeStruct(a5.shape, BF16),
        compiler_params=_SEQ2,
        name="fft_mid",
    )(a5, h, m_fwd, m_inv)


def _fft_filter_mid_kernel(a_ref, mf_ref, o_ref):
    for kk in range(SUBLANES):
        a = jnp.concatenate([a_ref[0, kk], a_ref[1, kk]], axis=0)
        o_ref[kk] = _dot(mf_ref[...], a)


def fft_filter_mid(a5, m_fwd):
    no, _, k2p, n1, c = a5.shape
    return pl.pallas_call(
        _fft_filter_mid_kernel,
        grid=(no, k2p // SUBLANES),
        in_specs=[pl.BlockSpec((None, 2, SUBLANES, n1, c), lambda o, j: (o, 0, j, 0, 0)),
                  pl.BlockSpec(m_fwd.shape, lambda o, j: (0, 0))],
        out_specs=pl.BlockSpec((None, SUBLANES, 2 * n1, c), lambda o, j: (o, j, 0, 0)),
        out_shape=jax.ShapeDtypeStruct((no, k2p, 2 * n1, c), F32),
        compiler_params=_SEQ2,
        name="fft_filter_mid",
    )(a5, m_fwd)


def _fft_stage3_kernel(b_ref, f_ref, u_ref, g_ref, skip_ref, o_ref, scr_ref, *, width):
    slabs = width // LANES
    slab_rows = 2 * FFT_K2_PAD * FFT_B_PITCH

    def row(r, carry):
        for part in range(2):
            tile = b_ref[part, r].astype(F32)
            for c in range(slabs):
                dst = pl.multiple_of(c * slab_rows + (part * FFT_K2_PAD + r) * FFT_B_PITCH, SUBLANES)
                scr_ref[pl.ds(dst, FFT_COLS), :] = tile[:, c * LANES:(c + 1) * LANES]
        return carry

    lax.fori_loop(0, FFT_K2_PAD, row, 0)
    for g in range(FFT_COLS):
        cols = slice(g * width, (g + 1) * width)
        spec = jnp.concatenate([scr_ref[pl.ds(c * slab_rows + g, 2 * FFT_K2_PAD, stride=FFT_B_PITCH), :]
                                for c in range(slabs)], axis=1).astype(BF16)
        y = _dot(f_ref[g], spec)
        o_ref[:, cols] = g_ref[:, cols] * (y + skip_ref[...] * u_ref[:, cols])


def fft_stage3(b5, f3, uv, gv, skip, *, width):
    nb, rows, cols = uv.shape
    blk = FFT_COLS * width
    view = pl.BlockSpec((None, rows, blk), lambda b, j: (b, 0, j))
    return pl.pallas_call(
        functools.partial(_fft_stage3_kernel, width=width),
        grid=(nb, FFT_N1 // FFT_COLS),
        in_specs=[pl.BlockSpec((None, 2, FFT_K2_PAD, FFT_COLS, width), lambda b, j: (b, 0, 0, j, 0)),
                  pl.BlockSpec((FFT_COLS, rows, 2 * FFT_K2_PAD), lambda b, j: (j, 0, 0)),
                  view, view, pl.BlockSpec((1, width), lambda b, j: (0, 0))],
        out_specs=view,
        out_shape=jax.ShapeDtypeStruct(uv.shape, F32),
        scratch_shapes=[pltpu.VMEM((width // LANES * 2 * FFT_K2_PAD * FFT_B_PITCH, LANES), F32)],
        compiler_params=_SEQ2,
        name="fft_stage3",
    )(b5, f3, uv, gv, skip.reshape(1, width))


def long_conv(uv, gv, h, skip, consts, *, width):
    f1, f3, m_fwd, m_inv = consts
    b5 = fft_mid(fft_stage1(uv, f1, width=width), h, m_fwd, m_inv)
    return fft_stage3(b5, f3, uv, gv, skip, width=width)


def filter_features(n):
    t = jnp.linspace(0.0, 1.0, n, dtype=F32)[:, None]
    w = (2.0 * math.pi / n) * jnp.arange(n, dtype=F32)[:, None]
    f = jnp.linspace(1e-4, HY_BANDS - 1, HY_BANDS, dtype=F32)[None, :]
    z = jnp.concatenate([t, jnp.cos(f * w), -jnp.sin(f * w)], axis=-1)
    z2 = jnp.concatenate([z, z[:1], z[:0:-1]], axis=0)
    return jnp.pad(z2, ((0, 0), (0, LANES - z2.shape[1])))


def _filter_mlp_kernel(z_ref, w1_ref, b1_ref, f1_ref, w2_ref, b2_ref, f2_ref, o_ref):
    a = jnp.sin(f1_ref[...] * (_dot(z_ref[...].astype(BF16), w1_ref[...].astype(BF16)) + b1_ref[...]))
    o_ref[...] = jnp.sin(f2_ref[...] * (_dot(a.astype(BF16), w2_ref[...].astype(BF16)) + b2_ref[...]))


def filter_mlp(z2, w1, b1, f1, w2, b2, f2, *, tr):
    rows = z2.shape[0]
    hid = w2.shape[0]
    w1p = jnp.pad(w1, ((0, LANES - w1.shape[0]), (0, 0)))
    vec = lambda v: v.reshape(1, hid)
    const = lambda i: (0, 0)
    return pl.pallas_call(
        _filter_mlp_kernel,
        grid=(rows // tr,),
        in_specs=[pl.BlockSpec((tr, LANES), lambda i: (i, 0)), pl.BlockSpec((LANES, hid), const),
                  pl.BlockSpec((1, hid), const), pl.BlockSpec((1, hid), const), pl.BlockSpec((hid, hid), const),
                  pl.BlockSpec((1, hid), const), pl.BlockSpec((1, hid), const)],
        out_specs=pl.BlockSpec((tr, hid), lambda i: (i, 0)),
        out_shape=jax.ShapeDtypeStruct((rows, hid), F32),
        name="filter_mlp",
    )(z2, w1p, vec(b1), vec(f1), w2, vec(b2), vec(f2))


def _filter_kernel(a_ref, t_ref, wf_ref, wb_ref, df_ref, db_ref, o_ref):
    n = a_ref.shape[0] // 2
    a = a_ref[...].astype(BF16)
    t = t_ref[...]
    hf = _dot(a[:n], wf_ref[...].astype(BF16)) * jnp.exp(-t[:n] * jnp.abs(df_ref[...]))
    hb = _dot(a[n:], wb_ref[...].astype(BF16)) * jnp.exp(-t[n:] * jnp.abs(db_ref[...]))
    row = lax.broadcasted_iota(jnp.int32, hb.shape, 0)
    hb0 = hb[0:1]
    hb = jnp.where(row == 0, 0.0, hb)
    norm = (jnp.sum(jnp.abs(hf), axis=0, keepdims=True) + jnp.sum(jnp.abs(hb), axis=0, keepdims=True)
            + jnp.abs(hb0) + 1e-6)
    hf = jnp.where(row == 0, hf + hb0, hf)
    o_ref[0:n, :] = hf / norm
    o_ref[n:, :] = hb / norm


def hyena_kernels(a2, z2, w3, decay, *, width, tc):
    rows, hid = a2.shape
    w3r = w3.reshape(hid, HY_ORDER, 2, width).transpose(1, 2, 0, 3)
    dec = decay.reshape(HY_ORDER, 2, 1, width)
    tcol = z2[:, 0:1]
    wspec = lambda s: pl.BlockSpec((None, None, hid, tc), lambda o, c: (o, s, 0, c))
    dspec = lambda s: pl.BlockSpec((None, None, 1, tc), lambda o, c: (o, s, 0, c))
    return pl.pallas_call(
        _filter_kernel,
        grid=(HY_ORDER, width // tc),
        in_specs=[pl.BlockSpec((rows, hid), lambda o, c: (0, 0)), pl.BlockSpec((rows, 1), lambda o, c: (0, 0)),
                  wspec(0), wspec(1), dspec(0), dspec(1)],
        out_specs=pl.BlockSpec((None, rows, tc), lambda o, c: (o, 0, c)),
        out_shape=jax.ShapeDtypeStruct((HY_ORDER, rows, width), F32),
        compiler_params=_SEQ2,
        name="hyena_filters",
    )(a2, tcol, w3r, w3r, dec, dec)


def filter_spectra(p, consts, *, n, width):
    f1, _, m_fwd, _ = consts
    z2 = filter_features(n)
    a2 = filter_mlp(z2, p['w1'], p['b1'], p['f1'], p['w2'], p['b2'], p['f2'], tr=1024)
    kc = hyena_kernels(a2, z2, p['w3'], p['decay'], width=width, tc=min(HY_FILTER_TILE, width))
    kv = kc.reshape(HY_ORDER, FFT_N2, FFT_N1 * width)
    return fft_filter_mid(fft_stage1(kv, f1, width=width), m_fwd)


def _hy_in_kernel(x_ref, xp_ref, xn_ref, sh_ref, sc_ref, g_ref, w_ref, cw_ref, cb_ref, v_ref, x1_ref, x2_ref, scr_ref):
    j = pl.program_id(1)
    nt = pl.num_programs(1)
    tm = x_ref.shape[0]
    width = w_ref.shape[1] // 3
    slabs = width // LANES
    blocks = tm // FFT_N1
    slab_rows = blocks * HY_VIEW_PITCH
    mod =lambda x: (_rms(x, g_ref[...]) * (1.0 + sc_ref[...]) + sh_ref[...]).astype(BF16)
    h = mod(x_ref[...])
    h_prev = mod(xp_ref[...])
    h_next = mod(xn_ref[...])
    keep_prev = jnp.where(j > 0, 1.0, 0.0)
    keep_next = jnp.where(j < nt - 1, 1.0, 0.0)
    row = lax.broadcasted_iota(jnp.int32, (tm, width), 0)
    cw = cw_ref[...]
    for part, o_ref in enumerate((v_ref, x1_ref, x2_ref)):
        cols = slice(part * width, (part + 1) * width)
        w = w_ref[:, cols]
        p = _dot(h, w)
        p_prev = _dot(h_prev, w)[SUBLANES - 1:SUBLANES] * keep_prev
        p_next = _dot(h_next, w)[0:1] * keep_next
        before = jnp.where(row == 0, p_prev, pltpu.roll(p, 1, 0))
        after = jnp.where(row == tm - 1, p_next, pltpu.roll(p, tm - 1, 0))
        res = cw[0:1, cols] * before + cw[1:2, cols] * p + cw[2:3, cols] * after + cb_ref[:, cols]
        for c in range(slabs):
            for blk in range(blocks):
                scr_ref[pl.ds(c * slab_rows + blk * HY_VIEW_PITCH, FFT_N1), :] = (
                    res[blk * FFT_N1:(blk + 1) * FFT_N1, c * LANES:(c + 1) * LANES])
        for n1 in range(FFT_N1):
            for c in range(slabs):
                lanes = slice(n1 * width + c * LANES, n1 * width + (c + 1) * LANES)
                o_ref[:, lanes] = scr_ref[pl.ds(c * slab_rows + n1, blocks, stride=HY_VIEW_PITCH), :]


def hy_in_proj(x, shift, scale, norm_g, w_in, conv_w, conv_b, *, n, tm):
    nb, _, d = x.shape
    width = w_in.shape[1] // 3
    rb = tm // SUBLANES
    last = n // SUBLANES - 1
    blocks = tm // FFT_N1
    tok = lambda b, j: (b, j, 0)
    mod_map = lambda b, j: (b, 0, 0)
    const = lambda b, j: (0, 0)
    out = pl.BlockSpec((None, blocks, FFT_N1 * width), tok)
    return pl.pallas_call(
        _hy_in_kernel,
        grid=(nb, n // tm),
        in_specs=[pl.BlockSpec((None, tm, d), tok),
                  pl.BlockSpec((None, SUBLANES, d), lambda b, j: (b, jnp.maximum(j * rb - 1, 0), 0)),
                  pl.BlockSpec((None, SUBLANES, d), lambda b, j: (b, jnp.minimum((j + 1) * rb, last), 0)),
                  pl.BlockSpec((None, 1, d), mod_map), pl.BlockSpec((None, 1, d), mod_map),
                  pl.BlockSpec((1, d), const), pl.BlockSpec(w_in.shape, const),
                  pl.BlockSpec(conv_w.shape, const), pl.BlockSpec((1, 3 * width), const)],
        out_specs=[out, out, out],
        out_shape=[jax.ShapeDtypeStruct((nb, n // FFT_N1, FFT_N1 * width), F32)] * 3,
        scratch_shapes=[pltpu.VMEM((width // LANES * blocks * HY_VIEW_PITCH, LANES), F32)],
        compiler_params=_SEQ2,
        name="hy_in_proj",
    )(x, x, x, shift, scale, norm_g.reshape(1, d), w_in.astype(BF16), conv_w, conv_b.reshape(1, -1))


def _hy_out_kernel(x_ref, gate_ref, z_ref, w_ref, o_ref, scr_ref):
    width = w_ref.shape[0]
    slabs = width // LANES
    blocks = z_ref.shape[0]
    slab_rows = blocks * HY_VIEW_PITCH
    for n1 in range(FFT_N1):
        for c in range(slabs):
            lanes = slice(n1 * width + c * LANES, n1 * width + (c + 1) * LANES)
            scr_ref[pl.ds(c * slab_rows + n1, blocks, stride=HY_VIEW_PITCH), :] = z_ref[:, lanes]
    z = jnp.concatenate(
        [jnp.concatenate([scr_ref[pl.ds(c * slab_rows + blk * HY_VIEW_PITCH, FFT_N1), :] for blk in range(blocks)],
                         axis=0) for c in range(slabs)], axis=1)
    o_ref[...] = x_ref[...] + gate_ref[...] * _dot(z.astype(BF16), w_ref[...])


def hy_out_proj(x, gate, zv, w_out, *, tm):
    nb, _, d = x.shape
    width = w_out.shape[0]
    n = zv.shape[1] * FFT_N1
    blocks = tm // FFT_N1
    tok = lambda b, j: (b, j, 0)
    return pl.pallas_call(
        _hy_out_kernel,
        grid=(nb, n // tm),
        in_specs=[pl.BlockSpec((None, tm, d), tok), pl.BlockSpec((None, 1, d), lambda b, j: (b, 0, 0)),
                  pl.BlockSpec((None, blocks, FFT_N1 * width), tok), pl.BlockSpec(w_out.shape, lambda b, j: (0, 0))],
        out_specs=pl.BlockSpec((None, tm, d), tok),
        out_shape=jax.ShapeDtypeStruct((nb, n, d), F32),
        scratch_shapes=[pltpu.VMEM((width // LANES * blocks * HY_VIEW_PITCH, LANES), F32)],
        compiler_params=_SEQ2,
        name="hy_out_proj",
    )(x, gate, zv, w_out.astype(BF16))


def hyena_mixer(x, shift, scale, gate, norm_g, p, *, n):
    width = p['w_in'].shape[1] // 3
    assert 2 * n == FFT_N
    consts = dft_constants()
    h = filter_spectra(p, consts, n=n, width=width)
    v, x1, x2 = hy_in_proj(x, shift, scale, norm_g, p['w_in'], p['conv_w'], p['conv_b'], n=n, tm=HY_IN_TILE)
    z = long_conv(v, x1, h[0], p['skip'][0], consts, width=width)
    z = long_conv(z, x2, h[1], p['skip'][1], consts, width=width)
    return hy_out_proj(x, gate, z, p['w_out'], tm=HY_IN_TILE)


def kernel(x, c, ctx, c_ctx, ada_w, ada_b, norm1_g, norm2_g, final_g, ev_w_in, mla_q_norm_g, mla_kv_norm_g, mla_w_uq, mla_w_ukv, lru_conv_w, lru_conv_b, lru_w_a, lru_b_a, lru_w_x, lru_b_x, lru_lambda, ev_w_out, od_w_in, hy_conv_w, hy_conv_b, hy_w1, hy_b1, hy_freq1, hy_w2, hy_b2, hy_freq2, hy_w3, hy_decay, hy_skip, od_w_out, router_w, router_b, moe_w1, moe_w3, moe_w2):
    nb, n_lat, d = x.shape
    n_ctx = ctx.shape[1]
    n_ctx_tiles = n_ctx // TOKEN_TILE
    cond = jnp.concatenate([jax.nn.silu(c), jax.nn.silu(c_ctx)[None, :]], axis=0)
    cond = jnp.pad(cond, ((0, 2 * SUBLANES - nb - 1), (0, 0)))

    def mod_rows(layer):
        mod = rows_matmul(cond, ada_w[layer], ada_b[layer], tn=D_MODEL)[:nb + 1]
        return [m[:, None, :] for m in jnp.split(mod, 6, axis=-1)]

    sh1, sc1, g1, sh2, sc2, g2 = mod_rows(0)
    x_all = jnp.concatenate([x, ctx], axis=1)
    p0 = dict(w_in=ev_w_in[0], q_norm_g=mla_q_norm_g[0], kv_norm_g=mla_kv_norm_g[0], w_uq=mla_w_uq[0],
              w_ukv=mla_w_ukv[0], conv_w=lru_conv_w[0], conv_b=lru_conv_b[0], w_a=lru_w_a[0], b_a=lru_b_a[0],
              w_x=lru_w_x[0], b_x=lru_b_x[0], lam=lru_lambda[0], w_out=ev_w_out[0])
    x_all = even_mixer(x_all, sh1, sc1, g1, norm1_g[0], p0, n_lat=n_lat, n_ctx=n_ctx)
    x_all = sparse_moe(x_all, sh2, sc2, g2, norm2_g[0], router_w, router_b, moe_w1[0], moe_w3[0], moe_w2[0],
                       final_g, n_ctx_tiles=n_ctx_tiles, final=False)

    sh1, sc1, g1, sh2, sc2, g2 = [m[:nb] for m in mod_rows(1)]
    p1 = dict(w_in=od_w_in[0], conv_w=hy_conv_w[0], conv_b=hy_conv_b[0], w1=hy_w1[0], b1=hy_b1[0], f1=hy_freq1[0],
              w2=hy_w2[0], b2=hy_b2[0], f2=hy_freq2[0], w3=hy_w3[0], decay=hy_decay[0], skip=hy_skip[0],
              w_out=od_w_out[0])
    x = hyena_mixer(x_all, sh1, sc1, g1, norm1_g[1], p1, n=n_lat)
    return sparse_moe(x, sh2, sc2, g2, norm2_g[1], router_w, router_b, moe_w1[1], moe_w3[1], moe_w2[1],
                      final_g, n_ctx_tiles=0, final=True)
```

```python
import functools
import math

import jax
import jax.numpy as jnp
import numpy as np
from jax import lax
from jax.experimental import pallas as pl
from jax.experimental.pallas import tpu as pltpu

D_MODEL = 1024
DEPTH = 2
GRID_W = 64
RMS_EPS = 1e-6

MLA_HEADS = 8
MLA_Q_LORA = 384
MLA_KV_LORA = 256
MLA_NOPE = 64
MLA_ROPE = 32
MLA_V = 64
MLA_QK = MLA_NOPE + MLA_ROPE
ROPE_PAIRS = MLA_ROPE // 4
ROPE_BASE = 10000.0

LRU_WIDTH = 512
LRU_BLOCKS = 8
LRU_BLOCK_DIM = LRU_WIDTH // LRU_BLOCKS
LRU_C = 8.0
LRU_CONV = 4

HY_WIDTH = D_MODEL
HY_ORDER = 2
HY_BANDS = 16
HY_HIDDEN = 64

N_EXPERTS = 16
N_GROUPS = 4
EXPERTS_PER_GROUP = N_EXPERTS // N_GROUPS

LANES = 128
SUBLANES = 8
HEAD_PAD = LANES

TOKEN_TILE = 256
MOE_ROW_TILE = 512
ATTN_Q_TILE = 1024
ATTN_KV_CHUNK = 1024
LRU_CHUNK = 256
LRU_SCAN_UNROLL = 8
DMA_LOOP_UNROLL = 8

BF16 = jnp.bfloat16
F32 = jnp.float32
_HI = lax.Precision.HIGHEST
_NEG_INF = float('-inf')


def _dot(a, b):
    return jnp.dot(a, b, preferred_element_type=F32)


def _rms(x, g):
    return x * lax.rsqrt(jnp.mean(x * x, axis=-1, keepdims=True) + RMS_EPS) * g


def _mod_map(nb, nt, n_ctx_tiles):
    if n_ctx_tiles:
        return lambda b, j: (jnp.where(j >= nt - n_ctx_tiles, nb, b), 0, 0)
    return lambda b, j: (b, 0, 0)


_SEQ2 = pltpu.CompilerParams(dimension_semantics=("arbitrary", "arbitrary"))


def _rows_matmul_kernel(x_ref, w_ref, b_ref, o_ref):
    o_ref[...] = _dot(x_ref[...].astype(BF16), w_ref[...].astype(BF16)) + b_ref[...]


def rows_matmul(x, w, b, *, tn):
    m, k = x.shape
    n = w.shape[1]
    return pl.pallas_call(
        _rows_matmul_kernel,
        grid=(n // tn,),
        in_specs=[pl.BlockSpec((m, k), lambda j: (0, 0)),
                  pl.BlockSpec((k, tn), lambda j: (0, j)),
                  pl.BlockSpec((1, tn), lambda j: (0, j))],
        out_specs=pl.BlockSpec((m, tn), lambda j: (0, j)),
        out_shape=jax.ShapeDtypeStruct((m, n), F32),
        name="rows_matmul",
    )(x, w, b.reshape(1, n))


def _rot_cols(w):
    p = ROPE_PAIRS
    return jnp.concatenate([-w[:, p:2 * p], w[:, 0:p], -w[:, 3 * p:4 * p], w[:, 2 * p:3 * p]], axis=1)


def _head_pad_cols(w, width):
    k = w.shape[0]
    w = w.reshape(k, MLA_HEADS, width)
    return jnp.pad(w, ((0, 0), (0, 0), (0, HEAD_PAD - width))).reshape(k, MLA_HEADS * HEAD_PAD)


def prep_even_weights(w_in, w_uq, w_ukv, w_out):
    d = w_in.shape[0]
    s1 = MLA_Q_LORA + MLA_KV_LORA
    s2 = s1 + MLA_ROPE
    w_kr = w_in[:, s1:s2]
    place = lambda w: jnp.pad(w, ((0, 0), (MLA_NOPE, HEAD_PAD - MLA_QK)))
    w_big = jnp.concatenate([w_in[:, :s1], w_in[:, s2:], place(w_kr), place(_rot_cols(w_kr))], axis=1).astype(BF16)
    uq = w_uq.reshape(MLA_Q_LORA, MLA_HEADS, MLA_QK)
    wq_a = _head_pad_cols(w_uq, MLA_QK)
    uq_rot = jnp.stack([_rot_cols(uq[:, h, MLA_NOPE:]) for h in range(MLA_HEADS)], axis=1)
    wq_b = jnp.pad(uq_rot, ((0, 0), (0, 0), (MLA_NOPE, HEAD_PAD - MLA_QK))).reshape(MLA_Q_LORA, MLA_HEADS * HEAD_PAD)
    ukv = w_ukv.reshape(MLA_KV_LORA, MLA_HEADS, MLA_NOPE + MLA_V)
    wk = _head_pad_cols(ukv[:, :, :MLA_NOPE].reshape(MLA_KV_LORA, -1), MLA_NOPE)
    wv = _head_pad_cols(ukv[:, :, MLA_NOPE:].reshape(MLA_KV_LORA, -1), MLA_V)
    w_kv = jnp.concatenate([wk, wv], axis=1)
    att_rows = MLA_HEADS * MLA_V
    wo_att = jnp.pad(w_out[:att_rows].reshape(MLA_HEADS, MLA_V, d),
                     ((0, 0), (0, HEAD_PAD - MLA_V), (0, 0))).reshape(MLA_HEADS * HEAD_PAD, d)
    return dict(w_big=w_big, wq_a=wq_a.astype(BF16), wq_b=wq_b.astype(BF16), w_kv=w_kv.astype(BF16),
                wo_att=wo_att.astype(BF16), wo_lru=w_out[att_rows:].astype(BF16))


def rope_tables(n_lat, n_ctx):
    rows = n_lat // GRID_W
    row = jnp.repeat(jnp.arange(rows), GRID_W)
    col = jnp.tile(jnp.arange(GRID_W), rows)
    inv_freq = ROPE_BASE ** (-jnp.arange(ROPE_PAIRS, dtype=F32) / ROPE_PAIRS)
    ang = jnp.stack([row, col], axis=-1).astype(F32)[:, :, None] * inv_freq
    cos, sin = jnp.cos(ang), jnp.sin(ang)
    c32 = jnp.concatenate([cos[:, 0], cos[:, 0], cos[:, 1], cos[:, 1]], axis=-1)
    s32 = jnp.concatenate([sin[:, 0], sin[:, 0], sin[:, 1], sin[:, 1]], axis=-1)
    c32 = jnp.concatenate([c32, jnp.ones((n_ctx, MLA_ROPE), F32)], axis=0)
    s32 = jnp.concatenate([s32, jnp.zeros((n_ctx, MLA_ROPE), F32)], axis=0)
    n = n_lat + n_ctx
    pad_hi = jnp.zeros((n, HEAD_PAD - MLA_QK), F32)
    scale = MLA_QK ** -0.5
    cq = jnp.concatenate([jnp.full((n, MLA_NOPE), scale, F32), c32 * scale, pad_hi], axis=-1)
    sq = jnp.concatenate([jnp.zeros((n, MLA_NOPE), F32), s32 * scale, pad_hi], axis=-1)
    ck = jnp.concatenate([jnp.zeros((n, MLA_NOPE), F32), c32, pad_hi], axis=-1)
    sk = jnp.concatenate([jnp.zeros((n, MLA_NOPE), F32), s32, pad_hi], axis=-1)
    return cq, sq, ck, sk


def _even_in_kernel(x_ref, sh_ref, sc_ref, g_ref, wbig_ref, qg_ref, kvg_ref, wqa_ref, wqb_ref, wkv_ref,
                    cq_ref, sq_ref, ck_ref, sk_ref, q_ref, k_ref, v_ref, ux_ref, ug_ref):
    h = (_rms(x_ref[...], g_ref[...]) * (1.0 + sc_ref[...]) + sh_ref[...]).astype(BF16)
    p = _dot(h, wbig_ref[...])
    o1 = MLA_Q_LORA
    o2 = o1 + MLA_KV_LORA
    o3 = o2 + LRU_WIDTH
    o4 = o3 + LRU_WIDTH
    o5 = o4 + HEAD_PAD
    ux_ref[...] = p[:, o2:o3]
    ug_ref[...] = p[:, o3:o4]
    cqn = _rms(p[:, :o1], qg_ref[...]).astype(BF16)
    rep = lambda t: jnp.concatenate([t] * MLA_HEADS, axis=-1)
    q = _dot(cqn, wqa_ref[...]) * rep(cq_ref[...]) + _dot(cqn, wqb_ref[...]) * rep(sq_ref[...])
    q_ref[...] = q.astype(BF16)
    ckvn = _rms(p[:, o1:o2], kvg_ref[...]).astype(BF16)
    kv = _dot(ckvn, wkv_ref[...])
    k_rope = p[:, o4:o5] * ck_ref[...] + p[:, o5:] * sk_ref[...]
    hw = MLA_HEADS * HEAD_PAD
    k_ref[...] = (kv[:, :hw] + rep(k_rope)).astype(BF16)
    lane = lax.broadcasted_iota(jnp.int32, (1, hw), 1)
    ones_col = jnp.where((lane & (HEAD_PAD - 1)) == MLA_V, 1.0, 0.0)
    v_ref[...] = (kv[:, hw:] + ones_col).astype(BF16)


def even_in_proj(x, shift, scale, norm_g, wts, q_norm_g, kv_norm_g, tables, *, tm, n_ctx_tiles):
    nb, s_len, d = x.shape
    nt = s_len // tm
    hw = MLA_HEADS * HEAD_PAD
    mod_map = _mod_map(nb, nt, n_ctx_tiles)
    const = lambda b, j: (0, 0)
    tok = lambda b, j: (b, j, 0)
    tab = lambda b, j: (j, 0)
    full = lambda a: pl.BlockSpec(a.shape, const)
    row = lambda v: v.reshape(1, -1)
    args = [x, shift, scale, row(norm_g), wts['w_big'], row(q_norm_g), row(kv_norm_g),
            wts['wq_a'], wts['wq_b'], wts['w_kv']]
    in_specs = [pl.BlockSpec((None, tm, d), tok), pl.BlockSpec((None, 1, d), mod_map),
                pl.BlockSpec((None, 1, d), mod_map)]
    in_specs += [full(a) for a in args[3:]]
    in_specs += [pl.BlockSpec((tm, HEAD_PAD), tab)] * 4
    out_dims = [(hw, BF16), (hw, BF16), (hw, BF16), (LRU_WIDTH, F32), (LRU_WIDTH, F32)]
    return pl.pallas_call(
        _even_in_kernel,
        grid=(nb, nt),
        in_specs=in_specs,
        out_specs=[pl.BlockSpec((None, tm, w), tok) for w, _ in out_dims],
        out_shape=[jax.ShapeDtypeStruct((nb, s_len, w), dt) for w, dt in out_dims],
        compiler_params=_SEQ2,
        name="even_in_proj",
    )(*args, *tables)


def _attn_kernel(q_ref, k_ref, v_ref, o_ref):
    q = q_ref[...]
    tq = q.shape[0]
    n_k = k_ref.shape[0]
    m = jnp.full((tq, 1), _NEG_INF, F32)
    acc = jnp.zeros((tq, HEAD_PAD), F32)
    for start in range(0, n_k, ATTN_KV_CHUNK):
        size = min(ATTN_KV_CHUNK, n_k - start)
        s = lax.dot_general(q, k_ref[start:start + size, :], (((1,), (1,)), ((), ())), preferred_element_type=F32)
        m_new = jnp.maximum(m, jnp.max(s, axis=-1, keepdims=True))
        p = jnp.exp(s - m_new).astype(BF16)
        acc = jnp.exp(m - m_new) * acc + _dot(p, v_ref[start:start + size, :])
        m = m_new
    o_ref[...] = (acc / acc[:, MLA_V:MLA_V + 1]).astype(o_ref.dtype)


def attention(q, k, v, *, q_start, n_q, k_start, n_k, tq):
    nb = q.shape[0]
    qo = q_start // tq
    ko = k_start // n_k
    return pl.pallas_call(
        _attn_kernel,
        grid=(nb, MLA_HEADS, n_q // tq),
        in_specs=[pl.BlockSpec((None, tq, HEAD_PAD), lambda b, h, i: (b, qo + i, h)),
                  pl.BlockSpec((None, n_k, HEAD_PAD), lambda b, h, i: (b, ko, h)),
                  pl.BlockSpec((None, n_k, HEAD_PAD), lambda b, h, i: (b, ko, h))],
        out_specs=pl.BlockSpec((None, tq, HEAD_PAD), lambda b, h, i: (b, i, h)),
        out_shape=jax.ShapeDtypeStruct((nb, n_q, MLA_HEADS * HEAD_PAD), BF16),
        compiler_params=pltpu.CompilerParams(dimension_semantics=("arbitrary", "arbitrary", "arbitrary")),
        name="attention",
    )(q, k, v)


def _scan_block(a, b, h_prev, row, reverse):
    for s in (1, 2, 4):
        sh = SUBLANES - s if reverse else s
        a_s = pltpu.roll(a, sh, 0)
        b_s = pltpu.roll(b, sh, 0)
        ok = (row < SUBLANES - s) if reverse else (row >= s)
        b = jnp.where(ok, a * b_s + b, b)
        a = jnp.where(ok, a * a_s, a)
    return a * h_prev + b


def _rglru_kernel(ux_ref, ug_ref, cw_ref, cb_ref, wa_ref, ba_ref, wx_ref, bx_ref, lam_ref, o_ref,
                  xp_ref, a_ref, b_ref, hf_ref, hb_ref, *, n_lat, n_ctx):
    lanes = ux_ref.shape[-1]
    pad = SUBLANES
    zeros = jnp.zeros((pad, lanes), F32)
    lat0 = pad
    ctx0 = 2 * pad + n_lat
    xp_ref[0:pad, :] = zeros
    xp_ref[lat0 + n_lat:ctx0, :] = zeros
    xp_ref[ctx0 + n_ctx:ctx0 + n_ctx + pad, :] = zeros
    xp_ref[lat0:lat0 + n_lat, :] = ux_ref[0:n_lat, :]
    xp_ref[ctx0:ctx0 + n_ctx, :] = ux_ref[n_lat:n_lat + n_ctx, :]
    cw = cw_ref[...]
    n_chunks = (n_lat + n_ctx) // LRU_CHUNK
    n_lat_chunks = n_lat // LRU_CHUNK
    n_win = LRU_CHUNK + 2 * pad

    def coeffs(c, carry):
        src = pl.multiple_of(c * LRU_CHUNK + jnp.where(c >= n_lat_chunks, pad, 0), SUBLANES)
        dst = pl.multiple_of(c * LRU_CHUNK, SUBLANES)
        win = xp_ref[pl.ds(src, n_win), :]
        mid = lambda t: t[pad:pad + LRU_CHUNK]
        u = cb_ref[...] + cw[2:3, :] * mid(win)
        u = u + cw[0:1, :] * mid(pltpu.roll(win, 2, 0))
        u = u + cw[1:2, :] * mid(pltpu.roll(win, 1, 0))
        u = u + cw[3:4, :] * mid(pltpu.roll(win, n_win - 1, 0))
        ub = u.astype(BF16)
        for d in range(2):
            r = jax.nn.sigmoid(_dot(ub, wa_ref[d]) + ba_ref[d])
            i = jax.nn.sigmoid(_dot(ub, wx_ref[d]) + bx_ref[d])
            a = jnp.exp(lam_ref[d] * r)
            a_ref[d, pl.ds(dst, LRU_CHUNK), :] = a
            b_ref[d, pl.ds(dst, LRU_CHUNK), :] = jnp.sqrt(1.0 - a * a) * (i * u)
        return carry

    lax.fori_loop(0, n_chunks, coeffs, 0)
    row = lax.broadcasted_iota(jnp.int32, (SUBLANES, lanes), 0)

    def make_step(first_blk, n_blk):
        def step(t, carry):
            h_f, h_b = carry
            rf = pl.multiple_of((first_blk + t) * SUBLANES, SUBLANES)
            rb = pl.multiple_of((first_blk + n_blk - 1 - t) * SUBLANES, SUBLANES)
            out_f = _scan_block(a_ref[0, pl.ds(rf, SUBLANES), :], b_ref[0, pl.ds(rf, SUBLANES), :], h_f, row, False)
            out_b = _scan_block(a_ref[1, pl.ds(rb, SUBLANES), :], b_ref[1, pl.ds(rb, SUBLANES), :], h_b, row, True)
            hf_ref[pl.ds(rf, SUBLANES), :] = out_f
            hb_ref[pl.ds(rb, SUBLANES), :] = out_b
            h_f = jnp.broadcast_to(out_f[SUBLANES - 1:SUBLANES, :], out_f.shape)
            h_b = jnp.broadcast_to(out_b[0:1, :], out_b.shape)
            return h_f, h_b
        return step

    state = (jnp.zeros((SUBLANES, lanes), F32), jnp.zeros((SUBLANES, lanes), F32))
    state = lax.fori_loop(0, n_ctx // SUBLANES, make_step(n_lat // SUBLANES, n_ctx // SUBLANES), state,
                          unroll=LRU_SCAN_UNROLL)
    lax.fori_loop(0, n_lat // SUBLANES, make_step(0, n_lat // SUBLANES), state, unroll=LRU_SCAN_UNROLL)

    def gate(c, carry):
        r0 = pl.multiple_of(c * LRU_CHUNK, SUBLANES)
        g = ug_ref[pl.ds(r0, LRU_CHUNK), :]
        gelu = 0.5 * g * (1.0 + jnp.tanh(math.sqrt(2.0 / math.pi) * (g + 0.044715 * (g * g * g))))
        y = hf_ref[pl.ds(r0, LRU_CHUNK), :] + hb_ref[pl.ds(r0, LRU_CHUNK), :]
        o_ref[pl.ds(r0, LRU_CHUNK), :] = (y * gelu).astype(o_ref.dtype)
        return carry

    lax.fori_loop(0, n_chunks, gate, 0)


def _block_diag(w):
    per = LANES // LRU_BLOCK_DIM
    nt = LRU_BLOCKS // per
    w = w.reshape(2, nt, per, LRU_BLOCK_DIM, LRU_BLOCK_DIM)
    eye = jnp.eye(per, dtype=w.dtype)
    out = jnp.einsum('dtpij,pq->dtpiqj', w, eye)
    return out.reshape(2, nt, LANES, LANES)


def rglru(ux, ug, conv_w, conv_b, w_a, b_a, w_x, b_x, lam, *, n_lat, n_ctx):
    nb, s_len, width = ux.shape
    nt = width // LANES
    wa = _block_diag(w_a).astype(BF16).transpose(1, 0, 2, 3)
    wx = _block_diag(w_x).astype(BF16).transpose(1, 0, 2, 3)
    lam_c = -LRU_C * jax.nn.softplus(-lam.astype(F32))
    lane3 = lambda v: v.reshape(2, 1, width)
    seq = pl.BlockSpec((None, s_len, LANES), lambda b, c: (b, 0, c))
    vec = lambda rows: pl.BlockSpec((rows, LANES), lambda b, c: (0, c))
    dvec = pl.BlockSpec((2, 1, LANES), lambda b, c: (0, 0, c))
    wspec = pl.BlockSpec((None, 2, LANES, LANES), lambda b, c: (c, 0, 0, 0))
    return pl.pallas_call(
        functools.partial(_rglru_kernel, n_lat=n_lat, n_ctx=n_ctx),
        grid=(nb, nt),
        in_specs=[seq, seq, vec(LRU_CONV), vec(1), wspec, dvec, wspec, dvec, dvec],
        out_specs=seq,
        out_shape=jax.ShapeDtypeStruct((nb, s_len, width), BF16),
        scratch_shapes=[pltpu.VMEM((s_len + 3 * SUBLANES, LANES), F32),
                        pltpu.VMEM((2, s_len, LANES), F32), pltpu.VMEM((2, s_len, LANES), F32),
                        pltpu.VMEM((s_len, LANES), F32), pltpu.VMEM((s_len, LANES), F32)],
        compiler_params=_SEQ2,
        name="rglru",
    )(ux, ug, conv_w, conv_b.reshape(1, width), wa, lane3(b_a), wx, lane3(b_x), lane3(lam_c))


def _even_out_kernel(x_ref, gate_ref, att_ref, lru_ref, wa_ref, wl_ref, o_ref):
    y = _dot(att_ref[...], wa_ref[...]) + _dot(lru_ref[...], wl_ref[...])
    o_ref[...] = x_ref[...] + gate_ref[...] * y


def even_out_proj(x, gate, att, lru, wo_att, wo_lru, *, tm, n_ctx_tiles):
    nb, s_len, d = x.shape
    nt = s_len // tm
    tok = lambda b, j: (b, j, 0)
    const = lambda b, j: (0, 0)
    return pl.pallas_call(
        _even_out_kernel,
        grid=(nb, nt),
        in_specs=[pl.BlockSpec((None, tm, d), tok), pl.BlockSpec((None, 1, d), _mod_map(nb, nt, n_ctx_tiles)),
                  pl.BlockSpec((None, tm, att.shape[-1]), tok), pl.BlockSpec((None, tm, lru.shape[-1]), tok),
                  pl.BlockSpec(wo_att.shape, const), pl.BlockSpec(wo_lru.shape, const)],
        out_specs=pl.BlockSpec((None, tm, d), tok),
        out_shape=jax.ShapeDtypeStruct((nb, s_len, d), F32),
        compiler_params=_SEQ2,
        name="even_out_proj",
    )(x, gate, att, lru, wo_att, wo_lru)


def even_mixer(x_all, shift, scale, gate, norm_g, p, *, n_lat, n_ctx):
    n_ctx_tiles = n_ctx // TOKEN_TILE
    tables = rope_tables(n_lat, n_ctx)
    wts = prep_even_weights(p['w_in'], p['w_uq'], p['w_ukv'], p['w_out'])
    q, k, v, ux, ug = even_in_proj(x_all, shift, scale, norm_g, wts, p['q_norm_g'], p['kv_norm_g'], tables,
                                   tm=TOKEN_TILE, n_ctx_tiles=n_ctx_tiles)
    att_l = attention(q, k, v, q_start=0, n_q=n_lat, k_start=0, n_k=n_lat + n_ctx, tq=ATTN_Q_TILE)
    att_c = attention(q, k, v, q_start=n_lat, n_q=n_ctx, k_start=n_lat, n_k=n_ctx, tq=n_ctx)
    att = jnp.concatenate([att_l, att_c], axis=1)
    lru = rglru(ux, ug, p['conv_w'], p['conv_b'], p['w_a'], p['b_a'], p['w_x'], p['b_x'], p['lam'],
                n_lat=n_lat, n_ctx=n_ctx)
    return even_out_proj(x_all, gate, att, lru, wts['wo_att'], wts['wo_lru'], tm=TOKEN_TILE, n_ctx_tiles=n_ctx_tiles)


def _load_token_tiles(ref, n_tok, d):
    return jnp.concatenate([ref[pl.ds(c, n_tok, stride=SUBLANES), :] for c in range(d // LANES)], axis=1)


def _store_token_tiles(ref, val):
    n_tok, d = val.shape
    for c in range(d // LANES):
        ref[pl.ds(c, n_tok, stride=SUBLANES), :] = val[:, c * LANES:(c + 1) * LANES]


def _first_argmax(vals):
    best = vals[0]
    idx = jnp.zeros(vals[0].shape, jnp.int32)
    for j in range(1, len(vals)):
        better = vals[j] > best
        idx = jnp.where(better, j, idx)
        best = jnp.where(better, vals[j], best)
    return idx, best


def _select(idx, vals):
    out = vals[0]
    for j in range(1, len(vals)):
        out = jnp.where(idx == j, vals[j], out)
    return out


def _moe_route_kernel(x_ref, sh_ref, sc_ref, g_ref, rwt_ref, rb_ref,
                      h_ref, e_ref, w_ref, rank_ref, cnt_ref, carry_ref, *, n_groups, per_group):
    first = jnp.logical_and(pl.program_id(0) == 0, pl.program_id(1) == 0)

    @pl.when(first)
    def _():
        carry_ref[...] = jnp.zeros_like(carry_ref)

    tm = x_ref.shape[0]
    n_exp = n_groups * per_group
    h = _rms(x_ref[...], g_ref[...]) * (1.0 + sc_ref[...]) + sh_ref[...]
    _store_token_tiles(h_ref, h)
    logits = lax.dot_general(rwt_ref[...], h, (((1,), (1,)), ((), ())),
                             precision=_HI, preferred_element_type=F32)
    s = jax.nn.sigmoid(logits)
    sel = s + rb_ref[...]
    sel_rows = [sel[e:e + 1, :] for e in range(n_exp)]
    s_rows = [s[e:e + 1, :] for e in range(n_exp)]
    g_scores = []
    for g in range(n_groups):
        r = sel_rows[g * per_group:(g + 1) * per_group]
        pair_sums = [r[a] + r[b] for a in range(per_group) for b in range(a + 1, per_group)]
        g_scores.append(functools.reduce(jnp.maximum, pair_sums))
    g_idx, _ = _first_argmax(g_scores)
    v = [_select(g_idx, [sel_rows[g * per_group + j] for g in range(n_groups)]) for j in range(per_group)]
    sv = [_select(g_idx, [s_rows[g * per_group + j] for g in range(n_groups)]) for j in range(per_group)]
    i1, _ = _first_argmax(v)
    i2, _ = _first_argmax([jnp.full_like(v[0], _NEG_INF)]
                          + [jnp.where(i1 == j, _NEG_INF, v[j]) for j in range(per_group)])
    i2 = i2 - 1
    w1 = _select(i1, sv)
    w2 = _select(i2, sv)
    wsum = w1 + w2
    e1 = g_idx * per_group + i1
    e2 = g_idx * per_group + i2
    e_ref[0:1, :] = e1
    e_ref[1:2, :] = e2
    w_ref[0:1, :] = w1 / wsum
    w_ref[1:2, :] = w2 / wsum
    e_iota = lax.broadcasted_iota(jnp.int32, (n_exp, tm), 0)
    oh1 = e_iota == e1
    oh2 = e_iota == e2
    m = jnp.where(oh1, 1.0, jnp.where(oh2, 1.0, 0.0))
    r_i = lax.broadcasted_iota(jnp.int32, (tm, tm), 0)
    c_i = lax.broadcasted_iota(jnp.int32, (tm, tm), 1)
    upper = jnp.where(r_i < c_i, 1.0, 0.0).astype(BF16)
    pre = _dot(m.astype(BF16), upper)
    tot = pre + carry_ref[:, 0:1]
    rank_ref[0:1, :] = jnp.sum(jnp.where(oh1, tot, 0.0), axis=0, keepdims=True).astype(jnp.int32)
    rank_ref[1:2, :] = jnp.sum(jnp.where(oh2, tot, 0.0), axis=0, keepdims=True).astype(jnp.int32)
    new_carry = carry_ref[...] + jnp.sum(m, axis=1, keepdims=True)
    carry_ref[...] = new_carry
    cnt_ref[...] = new_carry


def moe_route(x, shift, scale, norm_g, router_w, router_b, *, tm, n_ctx_tiles, n_groups, per_group):
    nb, s_len, d = x.shape
    nt = s_len // tm
    n_exp = n_groups * per_group
    rwt = router_w.T
    rb = router_b.reshape(n_exp, 1)
    mod_map = _mod_map(nb, nt, n_ctx_tiles)
    tok_map = lambda b, j: (b * nt + j, 0, 0)
    return pl.pallas_call(
        functools.partial(_moe_route_kernel, n_groups=n_groups, per_group=per_group),
        grid=(nb, nt),
        in_specs=[
            pl.BlockSpec((None, tm, d), lambda b, j: (b, j, 0)),
            pl.BlockSpec((None, 1, d), mod_map),
            pl.BlockSpec((None, 1, d), mod_map),
            pl.BlockSpec((1, d), lambda b, j: (0, 0)),
            pl.BlockSpec((n_exp, d), lambda b, j: (0, 0)),
            pl.BlockSpec((n_exp, 1), lambda b, j: (0, 0)),
        ],
        out_specs=[
            pl.BlockSpec((tm * SUBLANES, d // SUBLANES), lambda b, j: (b * nt + j, 0)),
            pl.BlockSpec((None, 2, tm), tok_map),
            pl.BlockSpec((None, 2, tm), tok_map),
            pl.BlockSpec((None, 2, tm), tok_map),
            pl.BlockSpec((n_exp, LANES), lambda b, j: (0, 0)),
        ],
        out_shape=[
            jax.ShapeDtypeStruct((nb * s_len * SUBLANES, d // SUBLANES), F32),
            jax.ShapeDtypeStruct((nb * nt, 2, tm), jnp.int32),
            jax.ShapeDtypeStruct((nb * nt, 2, tm), F32),
            jax.ShapeDtypeStruct((nb * nt, 2, tm), jnp.int32),
            jax.ShapeDtypeStruct((n_exp, LANES), F32),
        ],
        scratch_shapes=[pltpu.VMEM((n_exp, LANES), F32)],
        compiler_params=_SEQ2,
        name="moe_route",
    )(x, shift, scale, norm_g.reshape(1, d), rwt, rb)


def _row_copy(src_ref, src_tok, dst_ref, dst_tok, sem):
    src = src_ref.at[pl.ds(pl.multiple_of(src_tok * SUBLANES, SUBLANES), SUBLANES)]
    dst = dst_ref.at[pl.ds(pl.multiple_of(dst_tok * SUBLANES, SUBLANES), SUBLANES)]
    return pltpu.make_async_copy(src, dst, sem)


def _moe_dispatch_kernel(pos_ref, h_ref, xs_in_ref, xs_ref, sem):
    del xs_in_ref
    tm = h_ref.shape[0] // SUBLANES

    def start(r, c):
        _row_copy(h_ref, r, xs_ref, pos_ref[0, r], sem).start()
        _row_copy(h_ref, r, xs_ref, pos_ref[1, r], sem).start(priority=1)
        return c

    lax.fori_loop(0, tm, start, 0, unroll=DMA_LOOP_UNROLL)

    def wait(r, c):
        _row_copy(h_ref, 0, xs_ref, 0, sem).wait()
        _row_copy(h_ref, 0, xs_ref, 0, sem).wait()
        return c

    lax.fori_loop(0, tm, wait, 0, unroll=DMA_LOOP_UNROLL)


def moe_dispatch(h_tiles, pos, n_rows, *, tm):
    rows, lanes = h_tiles.shape
    zeros = jnp.zeros((n_rows * SUBLANES, lanes), h_tiles.dtype)
    return pl.pallas_call(
        _moe_dispatch_kernel,
        grid=(rows // (tm * SUBLANES),),
        in_specs=[
            pl.BlockSpec((None, 2, tm), lambda i: (i, 0, 0), memory_space=pltpu.SMEM),
            pl.BlockSpec((tm * SUBLANES, lanes), lambda i: (i, 0)),
            pl.BlockSpec(memory_space=pl.ANY),
        ],
        out_specs=pl.BlockSpec(memory_space=pl.ANY),
        out_shape=jax.ShapeDtypeStruct(zeros.shape, h_tiles.dtype),
        scratch_shapes=[pltpu.SemaphoreType.DMA(())],
        input_output_aliases={2: 0},
        compiler_params=pltpu.CompilerParams(dimension_semantics=("arbitrary",), has_side_effects=True),
        name="moe_dispatch",
    )(pos, h_tiles, zeros)


def _moe_expert_kernel(te_ref, nv_ref, xs_ref, w1_ref, w3_ref, w2_ref, ys_ref, w1b_ref, w3b_ref, w2b_ref):
    i = pl.program_id(0)
    live = i < nv_ref[0]

    @pl.when(jnp.logical_and(live, jnp.logical_or(i == 0, te_ref[i] != te_ref[jnp.maximum(i - 1, 0)])))
    def _():
        w1b_ref[...] = w1_ref[...].astype(BF16)
        w3b_ref[...] = w3_ref[...].astype(BF16)
        w2b_ref[...] = w2_ref[...].astype(BF16)

    @pl.when(live)
    def _():
        d = w1_ref.shape[0]
        x = _load_token_tiles(xs_ref, xs_ref.shape[0] // SUBLANES, d).astype(BF16)
        a = _dot(x, w1b_ref[...])
        b = _dot(x, w3b_ref[...])
        act = (a * jax.nn.sigmoid(a) * b).astype(BF16)
        _store_token_tiles(ys_ref, _dot(act, w2b_ref[...]))

    @pl.when(jnp.logical_not(live))
    def _():
        ys_ref[...] = jnp.zeros_like(ys_ref)


def moe_experts(xs, tile_expert, n_valid, w1, w3, w2, *, layer, tmm):
    rows, lanes = xs.shape
    n_tiles = rows // (tmm * SUBLANES)
    d, ff = w1.shape[-2:]
    row_map = lambda i, te, nv: (jnp.minimum(i, nv[0] - 1), 0)
    w_map = lambda i, te, nv: (layer, te[jnp.minimum(i, nv[0] - 1)], 0, 0)
    grid_spec = pltpu.PrefetchScalarGridSpec(
        num_scalar_prefetch=2,
        grid=(n_tiles,),
        in_specs=[
            pl.BlockSpec((tmm * SUBLANES, lanes), row_map),
            pl.BlockSpec((None, None, d, ff), w_map),
            pl.BlockSpec((None, None, d, ff), w_map),
            pl.BlockSpec((None, None, ff, d), w_map),
        ],
        out_specs=pl.BlockSpec((tmm * SUBLANES, lanes), lambda i, te, nv: (i, 0)),
        scratch_shapes=[pltpu.VMEM((d, ff), BF16), pltpu.VMEM((d, ff), BF16), pltpu.VMEM((ff, d), BF16)],
    )
    return pl.pallas_call(
        _moe_expert_kernel,
        grid_spec=grid_spec,
        out_shape=jax.ShapeDtypeStruct(xs.shape, F32),
        compiler_params=pltpu.CompilerParams(dimension_semantics=("arbitrary",)),
        name="moe_experts",
    )(tile_expert, n_valid, xs, w1, w3, w2)


def _moe_combine_kernel(pos_ref, x_ref, gate_ref, w_ref, fg_ref, ys_ref, o_ref, buf0_ref, buf1_ref, sem, *, final):
    tm, d = x_ref.shape

    def start(r, c):
        _row_copy(ys_ref, pos_ref[0, r], buf0_ref, r, sem).start()
        _row_copy(ys_ref, pos_ref[1, r], buf1_ref, r, sem).start(priority=1)
        return c

    lax.fori_loop(0, tm, start, 0, unroll=DMA_LOOP_UNROLL)

    def wait(r, c):
        _row_copy(ys_ref, 0, buf0_ref, 0, sem).wait()
        _row_copy(ys_ref, 0, buf1_ref, 0, sem).wait()
        return c

    lax.fori_loop(0, tm, wait, 0, unroll=DMA_LOOP_UNROLL)
    w = w_ref[...]
    y = w[:, 0:1] * _load_token_tiles(buf0_ref, tm, d) + w[:, 1:2] * _load_token_tiles(buf1_ref, tm, d)
    out = x_ref[...] + gate_ref[...] * y
    o_ref[...] = _rms(out, fg_ref[...]) if final else out


def moe_combine(x, gate, wts, pos, ys, final_g, *, tm, n_ctx_tiles, final):
    nb, s_len, d = x.shape
    nt = s_len // tm
    return pl.pallas_call(
        functools.partial(_moe_combine_kernel, final=final),
        grid=(nb, nt),
        in_specs=[
            pl.BlockSpec((None, 2, tm), lambda b, j: (b * nt + j, 0, 0), memory_space=pltpu.SMEM),
            pl.BlockSpec((None, tm, d), lambda b, j: (b, j, 0)),
            pl.BlockSpec((None, 1, d), _mod_map(nb, nt, n_ctx_tiles)),
            pl.BlockSpec((tm, 2), lambda b, j: (b * nt + j, 0)),
            pl.BlockSpec((1, d), lambda b, j: (0, 0)),
            pl.BlockSpec(memory_space=pl.ANY),
        ],
        out_specs=pl.BlockSpec((None, tm, d), lambda b, j: (b, j, 0)),
        out_shape=jax.ShapeDtypeStruct((nb, s_len, d), F32),
        scratch_shapes=[pltpu.VMEM((tm * SUBLANES, d // SUBLANES), F32), pltpu.VMEM((tm * SUBLANES, d // SUBLANES), F32),
                        pltpu.SemaphoreType.DMA(())],
        compiler_params=_SEQ2,
        name="moe_combine",
    )(pos, x, gate, wts, final_g.reshape(1, d), ys)


def sparse_moe(x, shift, scale, gate, norm_g, router_w, router_b, w1, w3, w2, final_g, *, layer, n_ctx_tiles, final):
    nb, s_len, d = x.shape
    assert d == SUBLANES * LANES
    tm, tmm = TOKEN_TILE, MOE_ROW_TILE
    t = nb * s_len
    h, e, w, rank, cnt = moe_route(x, shift, scale, norm_g, router_w, router_b, tm=tm, n_ctx_tiles=n_ctx_tiles,
                                   n_groups=N_GROUPS, per_group=EXPERTS_PER_GROUP)
    counts = cnt[:, 0].astype(jnp.int32)
    padded = ((counts + tmm - 1) // tmm) * tmm
    ends = jnp.cumsum(padded)
    offs = ends - padded
    pos = offs[e] + rank
    n_tiles = (2 * t + N_EXPERTS * (tmm - 1)) // tmm + 1
    n_valid = (ends[-1] // tmm).astype(jnp.int32).reshape(1)
    tile_start = jnp.arange(n_tiles, dtype=jnp.int32) * tmm
    tile_expert = jnp.minimum(jnp.sum((tile_start[:, None] >= ends[None, :]).astype(jnp.int32), axis=1), N_EXPERTS - 1)
    xs = moe_dispatch(h, pos, n_tiles * tmm, tm=tm)
    ys = moe_experts(xs, tile_expert, n_valid, w1, w3, w2, layer=layer, tmm=tmm)
    wts = w.transpose(0, 2, 1).reshape(t, 2)
    return moe_combine(x, gate, wts, pos, ys, final_g, tm=tm, n_ctx_tiles=n_ctx_tiles, final=final)


FFT_N1 = 64
FFT_N2 = 128
FFT_N = FFT_N1 * FFT_N2
FFT_K2 = FFT_N2 // 2 + 1
FFT_K2_PAD = 72
FFT_COLS = 16
FFT_A_PITCH = 152
FFT_B_PITCH = 24
HY_VIEW_PITCH = 72
HY_IN_TILE = 512
HY_FILTER_TILE = 256


def dft_constants():
    n1 = np.arange(FFT_N1)
    n2 = np.arange(FFT_N2)
    k2 = np.arange(FFT_K2_PAD)
    n = n1[:, None, None] + FFT_N1 * n2[None, None, :]
    ang = 2.0 * np.pi * ((k2[None, :, None] * n) % FFT_N) / FFT_N
    live = (k2 < FFT_K2)[None, :, None]
    f1 = np.concatenate([np.cos(ang) * live, -np.sin(ang) * live], axis=1)
    wgt = np.where((k2 == 0) | (k2 == FFT_N2 // 2), 1.0, 2.0) * (k2 < FFT_K2) / FFT_N
    ang3 = np.transpose(ang, (0, 2, 1))
    f3 = np.concatenate([np.cos(ang3) * wgt, -np.sin(ang3) * wgt], axis=2)
    f3 = f3[:, :FFT_N2 // 2]
    a = 2.0 * np.pi * ((n1[:, None] * n1[None, :]) % FFT_N1) / FFT_N1
    cr, ci = np.cos(a), -np.sin(a)
    m_fwd = np.block([[cr, -ci], [ci, cr]])
    m_inv = np.block([[cr, ci], [-ci, cr]])
    return (jnp.asarray(f1, BF16), jnp.asarray(f3, BF16), jnp.asarray(m_fwd, BF16), jnp.asarray(m_inv, BF16))


def _fft_stage1_kernel(x_ref, f_ref, o_ref, scr_ref, *, width):
    slabs = width // LANES
    slab_rows = FFT_COLS * FFT_A_PITCH
    for g in range(FFT_COLS):
        res = _dot(f_ref[g], x_ref[:, g * width:(g + 1) * width].astype(BF16))
        for c in range(slabs):
            scr_ref[pl.ds(c * slab_rows + g * FFT_A_PITCH, 2 * FFT_K2_PAD), :] = res[:, c * LANES:(c + 1) * LANES]

    def row(r, carry):
        for part in range(2):
            for c in range(slabs):
                base = c * slab_rows + part * FFT_K2_PAD + r
                lo = scr_ref[pl.ds(base, SUBLANES, stride=FFT_A_PITCH), :]
                hi = scr_ref[pl.ds(base + SUBLANES * FFT_A_PITCH, SUBLANES, stride=FFT_A_PITCH), :]
                o_ref[part, r, :, c * LANES:(c + 1) * LANES] = jnp.concatenate([lo, hi], axis=0).astype(o_ref.dtype)
        return carry

    lax.fori_loop(0, FFT_K2_PAD, row, 0)


def fft_stage1(xv, f1, *, width):
    nb, k, cols = xv.shape
    blk = FFT_COLS * width
    return pl.pallas_call(
        functools.partial(_fft_stage1_kernel, width=width),
        grid=(nb, FFT_N1 // FFT_COLS),
        in_specs=[pl.BlockSpec((None, k, blk), lambda b, j: (b, 0, j)),
                  pl.BlockSpec((FFT_COLS, 2 * FFT_K2_PAD, k), lambda b, j: (j, 0, 0))],
        out_specs=pl.BlockSpec((None, 2, FFT_K2_PAD, FFT_COLS, width), lambda b, j: (b, 0, 0, j, 0)),
        out_shape=jax.ShapeDtypeStruct((nb, 2, FFT_K2_PAD, FFT_N1, width), BF16),
        scratch_shapes=[pltpu.VMEM((width // LANES * FFT_COLS * FFT_A_PITCH, LANES), F32)],
        compiler_params=_SEQ2,
        name="fft_stage1",
    )(xv, f1[:, :, :k])


def _fft_mid_kernel(a_ref, h_ref, mf_ref, mi_ref, o_ref):
    for kk in range(SUBLANES):
        a = jnp.concatenate([a_ref[0, kk], a_ref[1, kk]], axis=0)
        x = _dot(mf_ref[...], a)
        xr, xi = x[:FFT_N1], x[FFT_N1:]
        hr, hi = h_ref[kk, :FFT_N1], h_ref[kk, FFT_N1:]
        y = jnp.concatenate([xr * hr - xi * hi, xr * hi + xi * hr], axis=0).astype(BF16)
        b = _dot(mi_ref[...], y)
        o_ref[0, kk] = b[:FFT_N1].astype(o_ref.dtype)
        o_ref[1, kk] = b[FFT_N1:].astype(o_ref.dtype)


def fft_mid(a5, h, m_fwd, m_inv):
    nb, _, k2p, n1, c = a5.shape
    blk = pl.BlockSpec((None, 2, SUBLANES, n1, c), lambda j, b: (b, 0, j, 0, 0))
    return pl.pallas_call(
        _fft_mid_kernel,
        grid=(k2p // SUBLANES, nb),
        in_specs=[blk, pl.BlockSpec((SUBLANES, 2 * n1, c), lambda j, b: (j, 0, 0)),
                  pl.BlockSpec(m_fwd.shape, lambda j, b: (0, 0)), pl.BlockSpec(m_inv.shape, lambda j, b: (0, 0))],
        out_specs=blk,
        out_shape=jax.ShapeDtypeStruct(a5.shape, BF16),
        compiler_params=_SEQ2,
        name="fft_mid",
    )(a5, h, m_fwd, m_inv)


def _fft_filter_mid_kernel(a_ref, mf_ref, o_ref):
    for kk in range(SUBLANES):
        a = jnp.concatenate([a_ref[0, kk], a_ref[1, kk]], axis=0)
        o_ref[kk] = _dot(mf_ref[...], a)


def fft_filter_mid(a5, m_fwd):
    no, _, k2p, n1, c = a5.shape
    return pl.pallas_call(
        _fft_filter_mid_kernel,
        grid=(no, k2p // SUBLANES),
        in_specs=[pl.BlockSpec((None, 2, SUBLANES, n1, c), lambda o, j: (o, 0, j, 0, 0)),
                  pl.BlockSpec(m_fwd.shape, lambda o, j: (0, 0))],
        out_specs=pl.BlockSpec((None, SUBLANES, 2 * n1, c), lambda o, j: (o, j, 0, 0)),
        out_shape=jax.ShapeDtypeStruct((no, k2p, 2 * n1, c), F32),
        compiler_params=_SEQ2,
        name="fft_filter_mid",
    )(a5, m_fwd)


def _fft_stage3_kernel(b_ref, f_ref, u_ref, g_ref, skip_ref, o_ref, scr_ref, *, width):
    slabs = width // LANES
    slab_rows = 2 * FFT_K2_PAD * FFT_B_PITCH

    def row(r, carry):
        for part in range(2):
            tile = b_ref[part, r].astype(F32)
            for c in range(slabs):
                dst = pl.multiple_of(c * slab_rows + (part * FFT_K2_PAD + r) * FFT_B_PITCH, SUBLANES)
                scr_ref[pl.ds(dst, FFT_COLS), :] = tile[:, c * LANES:(c + 1) * LANES]
        return carry

    lax.fori_loop(0, FFT_K2_PAD, row, 0)
    for g in range(FFT_COLS):
        cols = slice(g * width, (g + 1) * width)
        spec = jnp.concatenate([scr_ref[pl.ds(c * slab_rows + g, 2 * FFT_K2_PAD, stride=FFT_B_PITCH), :]
                                for c in range(slabs)], axis=1).astype(BF16)
        y = _dot(f_ref[g], spec)
        o_ref[:, cols] = g_ref[:, cols] * (y + skip_ref[...] * u_ref[:, cols])


def fft_stage3(b5, f3, uv, gv, skip, *, width):
    nb, rows, cols = uv.shape
    blk = FFT_COLS * width
    view = pl.BlockSpec((None, rows, blk), lambda b, j: (b, 0, j))
    return pl.pallas_call(
        functools.partial(_fft_stage3_kernel, width=width),
        grid=(nb, FFT_N1 // FFT_COLS),
        in_specs=[pl.BlockSpec((None, 2, FFT_K2_PAD, FFT_COLS, width), lambda b, j: (b, 0, 0, j, 0)),
                  pl.BlockSpec((FFT_COLS, rows, 2 * FFT_K2_PAD), lambda b, j: (j, 0, 0)),
                  view, view, pl.BlockSpec((1, width), lambda b, j: (0, 0))],
        out_specs=view,
        out_shape=jax.ShapeDtypeStruct(uv.shape, F32),
        scratch_shapes=[pltpu.VMEM((width // LANES * 2 * FFT_K2_PAD * FFT_B_PITCH, LANES), F32)],
        compiler_params=_SEQ2,
        name="fft_stage3",
    )(b5, f3, uv, gv, skip.reshape(1, width))


def long_conv(uv, gv, h, skip, consts, *, width):
    f1, f3, m_fwd, m_inv = consts
    b5 = fft_mid(fft_stage1(uv, f1, width=width), h, m_fwd, m_inv)
    return fft_stage3(b5, f3, uv, gv, skip, width=width)


def filter_features(n):
    t = jnp.linspace(0.0, 1.0, n, dtype=F32)[:, None]
    w = (2.0 * math.pi / n) * jnp.arange(n, dtype=F32)[:, None]
    f = jnp.linspace(1e-4, HY_BANDS - 1, HY_BANDS, dtype=F32)[None, :]
    z = jnp.concatenate([t, jnp.cos(f * w), -jnp.sin(f * w)], axis=-1)
    z2 = jnp.concatenate([z, z[:1], z[:0:-1]], axis=0)
    return jnp.pad(z2, ((0, 0), (0, LANES - z2.shape[1])))


def _filter_mlp_kernel(z_ref, w1_ref, b1_ref, f1_ref, w2_ref, b2_ref, f2_ref, o_ref):
    a = jnp.sin(f1_ref[...] * (_dot(z_ref[...].astype(BF16), w1_ref[...].astype(BF16)) + b1_ref[...]))
    o_ref[...] = jnp.sin(f2_ref[...] * (_dot(a.astype(BF16), w2_ref[...].astype(BF16)) + b2_ref[...]))


def filter_mlp(z2, w1, b1, f1, w2, b2, f2, *, tr):
    rows = z2.shape[0]
    hid = w2.shape[0]
    w1p = jnp.pad(w1, ((0, LANES - w1.shape[0]), (0, 0)))
    vec = lambda v: v.reshape(1, hid)
    const = lambda i: (0, 0)
    return pl.pallas_call(
        _filter_mlp_kernel,
        grid=(rows // tr,),
        in_specs=[pl.BlockSpec((tr, LANES), lambda i: (i, 0)), pl.BlockSpec((LANES, hid), const),
                  pl.BlockSpec((1, hid), const), pl.BlockSpec((1, hid), const), pl.BlockSpec((hid, hid), const),
                  pl.BlockSpec((1, hid), const), pl.BlockSpec((1, hid), const)],
        out_specs=pl.BlockSpec((tr, hid), lambda i: (i, 0)),
        out_shape=jax.ShapeDtypeStruct((rows, hid), F32),
        name="filter_mlp",
    )(z2, w1p, vec(b1), vec(f1), w2, vec(b2), vec(f2))


def _filter_kernel(a_ref, t_ref, wf_ref, wb_ref, df_ref, db_ref, o_ref):
    n = a_ref.shape[0] // 2
    a = a_ref[...].astype(BF16)
    t = t_ref[...]
    hf = _dot(a[:n], wf_ref[...].astype(BF16)) * jnp.exp(-t[:n] * jnp.abs(df_ref[...]))
    hb = _dot(a[n:], wb_ref[...].astype(BF16)) * jnp.exp(-t[n:] * jnp.abs(db_ref[...]))
    row = lax.broadcasted_iota(jnp.int32, hb.shape, 0)
    hb0 = hb[0:1]
    hb = jnp.where(row == 0, 0.0, hb)
    norm = (jnp.sum(jnp.abs(hf), axis=0, keepdims=True) + jnp.sum(jnp.abs(hb), axis=0, keepdims=True)
            + jnp.abs(hb0) + 1e-6)
    hf = jnp.where(row == 0, hf + hb0, hf)
    o_ref[0:n, :] = hf / norm
    o_ref[n:, :] = hb / norm


def hyena_kernels(a2, z2, w3, decay, *, width, tc):
    rows, hid = a2.shape
    w3r = w3.reshape(hid, HY_ORDER, 2, width).transpose(1, 2, 0, 3)
    dec = decay.reshape(HY_ORDER, 2, 1, width)
    tcol = z2[:, 0:1]
    wspec = lambda s: pl.BlockSpec((None, None, hid, tc), lambda o, c: (o, s, 0, c))
    dspec = lambda s: pl.BlockSpec((None, None, 1, tc), lambda o, c: (o, s, 0, c))
    return pl.pallas_call(
        _filter_kernel,
        grid=(HY_ORDER, width // tc),
        in_specs=[pl.BlockSpec((rows, hid), lambda o, c: (0, 0)), pl.BlockSpec((rows, 1), lambda o, c: (0, 0)),
                  wspec(0), wspec(1), dspec(0), dspec(1)],
        out_specs=pl.BlockSpec((None, rows, tc), lambda o, c: (o, 0, c)),
        out_shape=jax.ShapeDtypeStruct((HY_ORDER, rows, width), F32),
        compiler_params=_SEQ2,
        name="hyena_filters",
    )(a2, tcol, w3r, w3r, dec, dec)


def filter_spectra(p, consts, *, n, width):
    f1, _, m_fwd, _ = consts
    z2 = filter_features(n)
    a2 = filter_mlp(z2, p['w1'], p['b1'], p['f1'], p['w2'], p['b2'], p['f2'], tr=1024)
    kc = hyena_kernels(a2, z2, p['w3'], p['decay'], width=width, tc=min(HY_FILTER_TILE, width))
    kv = kc.reshape(HY_ORDER, FFT_N2, FFT_N1 * width)
    return fft_filter_mid(fft_stage1(kv, f1, width=width), m_fwd)


def _hy_in_kernel(x_ref, xp_ref, xn_ref, sh_ref, sc_ref, g_ref, w_ref, cw_ref, cb_ref, v_ref, x1_ref, x2_ref, scr_ref):
    j = pl.program_id(1)
    nt = pl.num_programs(1)
    tm = x_ref.shape[0]
    width = w_ref.shape[1] // 3
    slabs = width // LANES
    blocks = tm // FFT_N1
    slab_rows = blocks * HY_VIEW_PITCH
    mod =lambda x: (_rms(x, g_ref[...]) * (1.0 + sc_ref[...]) + sh_ref[...]).astype(BF16)
    h = mod(x_ref[...])
    h_prev = mod(xp_ref[...])
    h_next = mod(xn_ref[...])
    keep_prev = jnp.where(j > 0, 1.0, 0.0)
    keep_next = jnp.where(j < nt - 1, 1.0, 0.0)
    row = lax.broadcasted_iota(jnp.int32, (tm, width), 0)
    cw = cw_ref[...]
    for part, o_ref in enumerate((v_ref, x1_ref, x2_ref)):
        cols = slice(part * width, (part + 1) * width)
        w = w_ref[:, cols]
        p = _dot(h, w)
        p_prev = _dot(h_prev, w)[SUBLANES - 1:SUBLANES] * keep_prev
        p_next = _dot(h_next, w)[0:1] * keep_next
        before = jnp.where(row == 0, p_prev, pltpu.roll(p, 1, 0))
        after = jnp.where(row == tm - 1, p_next, pltpu.roll(p, tm - 1, 0))
        res = cw[0:1, cols] * before + cw[1:2, cols] * p + cw[2:3, cols] * after + cb_ref[:, cols]
        for c in range(slabs):
            for blk in range(blocks):
                scr_ref[pl.ds(c * slab_rows + blk * HY_VIEW_PITCH, FFT_N1), :] = (
                    res[blk * FFT_N1:(blk + 1) * FFT_N1, c * LANES:(c + 1) * LANES])
        for n1 in range(FFT_N1):
            for c in range(slabs):
                lanes = slice(n1 * width + c * LANES, n1 * width + (c + 1) * LANES)
                o_ref[:, lanes] = scr_ref[pl.ds(c * slab_rows + n1, blocks, stride=HY_VIEW_PITCH), :]


def hy_in_proj(x, shift, scale, norm_g, w_in, conv_w, conv_b, *, n, tm):
    nb, _, d = x.shape
    width = w_in.shape[1] // 3
    rb = tm // SUBLANES
    last = n // SUBLANES - 1
    blocks = tm // FFT_N1
    tok = lambda b, j: (b, j, 0)
    mod_map = lambda b, j: (b, 0, 0)
    const = lambda b, j: (0, 0)
    out = pl.BlockSpec((None, blocks, FFT_N1 * width), tok)
    return pl.pallas_call(
        _hy_in_kernel,
        grid=(nb, n // tm),
        in_specs=[pl.BlockSpec((None, tm, d), tok),
                  pl.BlockSpec((None, SUBLANES, d), lambda b, j: (b, jnp.maximum(j * rb - 1, 0), 0)),
                  pl.BlockSpec((None, SUBLANES, d), lambda b, j: (b, jnp.minimum((j + 1) * rb, last), 0)),
                  pl.BlockSpec((None, 1, d), mod_map), pl.BlockSpec((None, 1, d), mod_map),
                  pl.BlockSpec((1, d), const), pl.BlockSpec(w_in.shape, const),
                  pl.BlockSpec(conv_w.shape, const), pl.BlockSpec((1, 3 * width), const)],
        out_specs=[out, out, out],
        out_shape=[jax.ShapeDtypeStruct((nb, n // FFT_N1, FFT_N1 * width), F32)] * 3,
        scratch_shapes=[pltpu.VMEM((width // LANES * blocks * HY_VIEW_PITCH, LANES), F32)],
        compiler_params=_SEQ2,
        name="hy_in_proj",
    )(x, x, x, shift, scale, norm_g.reshape(1, d), w_in.astype(BF16), conv_w, conv_b.reshape(1, -1))


def _hy_out_kernel(x_ref, gate_ref, z_ref, w_ref, o_ref, scr_ref):
    width = w_ref.shape[0]
    slabs = width // LANES
    blocks = z_ref.shape[0]
    slab_rows = blocks * HY_VIEW_PITCH
    for n1 in range(FFT_N1):
        for c in range(slabs):
            lanes = slice(n1 * width + c * LANES, n1 * width + (c + 1) * LANES)
            scr_ref[pl.ds(c * slab_rows + n1, blocks, stride=HY_VIEW_PITCH), :] = z_ref[:, lanes]
    z = jnp.concatenate(
        [jnp.concatenate([scr_ref[pl.ds(c * slab_rows + blk * HY_VIEW_PITCH, FFT_N1), :] for blk in range(blocks)],
                         axis=0) for c in range(slabs)], axis=1)
    o_ref[...] = x_ref[...] + gate_ref[...] * _dot(z.astype(BF16), w_ref[...])


def hy_out_proj(x, gate, zv, w_out, *, tm):
    nb, _, d = x.shape
    width = w_out.shape[0]
    n = zv.shape[1] * FFT_N1
    blocks = tm // FFT_N1
    tok = lambda b, j: (b, j, 0)
    return pl.pallas_call(
        _hy_out_kernel,
        grid=(nb, n // tm),
        in_specs=[pl.BlockSpec((None, tm, d), tok), pl.BlockSpec((None, 1, d), lambda b, j: (b, 0, 0)),
                  pl.BlockSpec((None, blocks, FFT_N1 * width), tok), pl.BlockSpec(w_out.shape, lambda b, j: (0, 0))],
        out_specs=pl.BlockSpec((None, tm, d), tok),
        out_shape=jax.ShapeDtypeStruct((nb, n, d), F32),
        scratch_shapes=[pltpu.VMEM((width // LANES * blocks * HY_VIEW_PITCH, LANES), F32)],
        compiler_params=_SEQ2,
        name="hy_out_proj",
    )(x, gate, zv, w_out.astype(BF16))


def hyena_mixer(x, shift, scale, gate, norm_g, p, *, n):
    width = p['w_in'].shape[1] // 3
    assert 2 * n == FFT_N
    consts = dft_constants()
    h = filter_spectra(p, consts, n=n, width=width)
    v, x1, x2 = hy_in_proj(x, shift, scale, norm_g, p['w_in'], p['conv_w'], p['conv_b'], n=n, tm=HY_IN_TILE)
    z = long_conv(v, x1, h[0], p['skip'][0], consts, width=width)
    z = long_conv(z, x2, h[1], p['skip'][1], consts, width=width)
    return hy_out_proj(x, gate, z, p['w_out'], tm=HY_IN_TILE)


def kernel(x, c, ctx, c_ctx, ada_w, ada_b, norm1_g, norm2_g, final_g, ev_w_in, mla_q_norm_g, mla_kv_norm_g, mla_w_uq, mla_w_ukv, lru_conv_w, lru_conv_b, lru_w_a, lru_b_a, lru_w_x, lru_b_x, lru_lambda, ev_w_out, od_w_in, hy_conv_w, hy_conv_b, hy_w1, hy_b1, hy_freq1, hy_w2, hy_b2, hy_freq2, hy_w3, hy_decay, hy_skip, od_w_out, router_w, router_b, moe_w1, moe_w3, moe_w2):
    nb, n_lat, d = x.shape
    n_ctx = ctx.shape[1]
    n_ctx_tiles = n_ctx // TOKEN_TILE
    cond = jnp.concatenate([jax.nn.silu(c), jax.nn.silu(c_ctx)[None, :]], axis=0)
    cond = jnp.pad(cond, ((0, 2 * SUBLANES - nb - 1), (0, 0)))

    def mod_rows(layer):
        mod = rows_matmul(cond, ada_w[layer], ada_b[layer], tn=D_MODEL)[:nb + 1]
        return [m[:, None, :] for m in jnp.split(mod, 6, axis=-1)]

    sh1, sc1, g1, sh2, sc2, g2 = mod_rows(0)
    x_all = jnp.concatenate([x, ctx], axis=1)
    p0 = dict(w_in=ev_w_in[0], q_norm_g=mla_q_norm_g[0], kv_norm_g=mla_kv_norm_g[0], w_uq=mla_w_uq[0],
              w_ukv=mla_w_ukv[0], conv_w=lru_conv_w[0], conv_b=lru_conv_b[0], w_a=lru_w_a[0], b_a=lru_b_a[0],
              w_x=lru_w_x[0], b_x=lru_b_x[0], lam=lru_lambda[0], w_out=ev_w_out[0])
    x_all = even_mixer(x_all, sh1, sc1, g1, norm1_g[0], p0, n_lat=n_lat, n_ctx=n_ctx)
    x_all = sparse_moe(x_all, sh2, sc2, g2, norm2_g[0], router_w, router_b, moe_w1, moe_w3, moe_w2,
                       final_g, layer=0, n_ctx_tiles=n_ctx_tiles, final=False)

    sh1, sc1, g1, sh2, sc2, g2 = [m[:nb] for m in mod_rows(1)]
    p1 = dict(w_in=od_w_in[0], conv_w=hy_conv_w[0], conv_b=hy_conv_b[0], w1=hy_w1[0], b1=hy_b1[0], f1=hy_freq1[0],
              w2=hy_w2[0], b2=hy_b2[0], f2=hy_freq2[0], w3=hy_w3[0], decay=hy_decay[0], skip=hy_skip[0],
              w_out=od_w_out[0])
    x = hyena_mixer(x_all, sh1, sc1, g1, norm1_g[1], p1, n=n_lat)
    return sparse_moe(x, sh2, sc2, g2, norm2_g[1], router_w, router_b, moe_w1, moe_w3, moe_w2,
                      final_g, layer=1, n_ctx_tiles=0, final=True)
```
